```python
import math
import jax
import jax.numpy as jnp
from jax import lax
import numpy as np

D_MODEL = 2048
BATCH = 1
SEQ = 8192
DEPTH = 2
DEC_BATCH = 128
DEC_SEQ = 4
PAST_LEN = 2048
PAGE_SIZE = 128

HEAD_DIM = 128
H_GDN = D_MODEL // 512
H_MOBA = D_MODEL // 256
H_RET = D_MODEL // 512
W_GDN = H_GDN * HEAD_DIM
W_MOBA = H_MOBA * HEAD_DIM
W_RET = H_RET * HEAD_DIM
MIX_WIDTH = W_GDN + W_MOBA + W_RET
IN_DIM = 4 * W_GDN + 2 * H_GDN + 3 * W_MOBA + 4 * W_RET
GDN_CONV = 4
GDN_CHUNK = 64
RET_CHUNK = 64
MOBA_BLOCK = 256
MOBA_TOPK = 3
MOBA_ROW_CHUNK = 64
NUM_BUCKETS = 32
MAX_DISTANCE = 128
D_FF = 5632
FFN_CONV = 3
RMS_EPS = 1e-6
NEG_INF = -1e30

kernel_name = 'moba_gdn_retention_hybrid_step'


def rms_norm(x, gain=None):
    x32 = x.astype(jnp.float32)
    y = x32 * lax.rsqrt(jnp.mean(x32 * x32, axis=-1, keepdims=True) + RMS_EPS)
    if gain is not None:
        y = y * gain.astype(jnp.float32)
    return y.astype(x.dtype)


def l2_normalize(x):
    return x * lax.rsqrt(jnp.sum(x * x, axis=-1, keepdims=True) + 1e-6)


def causal_dwconv(xcat, w):
    return lax.conv_general_dilated(
        xcat, w.astype(xcat.dtype)[:, None, :], window_strides=(1,), padding='VALID',
        dimension_numbers=('NWC', 'WIO', 'NWC'), feature_group_count=xcat.shape[-1])


def rotary(x, pos):
    half = x.shape[-1] // 2
    inv = 1.0 / (10000.0 ** jnp.linspace(0.0, 1.0, half, dtype=jnp.float32))
    ang = pos.astype(jnp.float32)[:, None] * inv[None, :]
    cos = jnp.cos(ang)[None, :, None, :]
    sin = jnp.sin(ang)[None, :, None, :]
    x1, x2 = x[..., :half], x[..., half:]
    return jnp.concatenate([x1 * cos - x2 * sin, x1 * sin + x2 * cos], axis=-1)


def t5_bucket(dist):
    max_exact = NUM_BUCKETS // 2
    d = jnp.maximum(dist, max_exact).astype(jnp.float32)
    large = max_exact + (jnp.log(d / max_exact) / math.log(MAX_DISTANCE / max_exact)
                         * (NUM_BUCKETS - max_exact)).astype(jnp.int32)
    large = jnp.minimum(large, NUM_BUCKETS - 1)
    return jnp.where(dist < max_exact, dist, large)


def to_chunks(t, n, c):
    b, _, h = t.shape[:3]
    t = t.reshape((b, n, c, h) + t.shape[3:])
    return jnp.moveaxis(t, (1, 3), (0, 2))


def from_chunks(t):
    t = jnp.moveaxis(t, (0, 2), (1, 3))
    return t.reshape(t.shape[0], t.shape[1] * t.shape[2], t.shape[3], t.shape[4])


def chunk_gated_delta(q, k, v, beta, g, s0):
    T = q.shape[1]
    c = math.gcd(T, GDN_CHUNK)
    n = T // c
    qc, kc, vc = to_chunks(q, n, c), to_chunks(k, n, c), to_chunks(v, n, c)
    bc = to_chunks(beta, n, c)
    gc = jnp.cumsum(to_chunks(g, n, c), axis=-1)
    lower = jnp.tril(jnp.ones((c, c), bool))
    strict = jnp.tril(jnp.ones((c, c), bool), -1)
    diff = gc[..., :, None] - gc[..., None, :]
    decay = jnp.where(lower, jnp.exp(jnp.where(lower, diff, 0.0)), 0.0)
    kb = kc * bc[..., None]
    lmat = jnp.where(strict, jnp.einsum('nbhid,nbhjd->nbhij', kb, kc) * decay, 0.0)
    eye = jnp.eye(c, dtype=jnp.float32)
    tinv = lax.linalg.triangular_solve(eye + lmat, jnp.broadcast_to(eye, lmat.shape),
                                       left_side=True, lower=True)
    u = jnp.einsum('nbhij,nbhje->nbhie', tinv, vc * bc[..., None])
    w = jnp.einsum('nbhij,nbhjd->nbhid', tinv, kb * jnp.exp(gc)[..., None])
    qk = jnp.einsum('nbhid,nbhjd->nbhij', qc, kc) * decay

    def step(s, inp):
        q_n, k_n, u_n, w_n, qk_n, g_n = inp
        v_new = u_n - jnp.einsum('bhcd,bhde->bhce', w_n, s)
        o = (jnp.einsum('bhcd,bhde->bhce', q_n * jnp.exp(g_n)[..., None], s)
             + jnp.einsum('bhij,bhje->bhie', qk_n, v_new))
        g_last = g_n[..., -1:]
        s = (s * jnp.exp(g_last)[..., None]
             + jnp.einsum('bhcd,bhce->bhde', k_n * jnp.exp(g_last - g_n)[..., None], v_new))
        return s, o

    s_fin, o = lax.scan(step, s0, (qc, kc, u, w, qk, gc))
    return from_chunks(o), s_fin


def chunk_retention(q, k, v, s0):
    T, H = q.shape[1], q.shape[2]
    c = math.gcd(T, RET_CHUNK)
    n = T // c
    lg = jnp.log(1.0 - 2.0 ** (-5.0 - jnp.arange(H, dtype=jnp.float32)))
    idx = jnp.arange(c, dtype=jnp.float32)
    rel = idx[:, None] - idx[None, :]
    dmat = jnp.where(rel >= 0, jnp.exp(jnp.maximum(rel, 0.0)[None] * lg[:, None, None]), 0.0)
    xi = jnp.exp((idx[None, :] + 1.0) * lg[:, None])
    zeta = jnp.exp((c - 1.0 - idx[None, :]) * lg[:, None])
    gamma_c = jnp.exp(c * lg)[:, None, None]
    qc, kc, vc = to_chunks(q, n, c), to_chunks(k, n, c), to_chunks(v, n, c)
    inner = jnp.einsum('nbhij,nbhje->nbhie', jnp.einsum('nbhid,nbhjd->nbhij', qc, kc) * dmat, vc)

    def step(s, inp):
        q_n, k_n, v_n, in_n = inp
        o = in_n + jnp.einsum('bhcd,bhde->bhce', q_n * xi[..., None], s)
        s = gamma_c * s + jnp.einsum('bhcd,bhce->bhde', k_n * zeta[..., None], v_n)
        return s, o

    s_fin, o = lax.scan(step, s0, (qc, kc, vc, inner))
    return from_chunks(o), s_fin


def gdn_mixer(qkv_raw, z, a_raw, b_raw, conv_buf, s0, conv_w, a_log, dt_bias, norm_w):
    bsz, T, _ = qkv_raw.shape
    xcat = jnp.concatenate([conv_buf.astype(qkv_raw.dtype), qkv_raw], axis=1)
    new_buf = xcat[:, -(GDN_CONV - 1):]
    qkv = jax.nn.silu(causal_dwconv(xcat, conv_w)).astype(jnp.float32)
    q, k, v = jnp.split(qkv, 3, axis=-1)
    q = l2_normalize(q.reshape(bsz, T, H_GDN, HEAD_DIM)) * HEAD_DIM ** -0.5
    k = l2_normalize(k.reshape(bsz, T, H_GDN, HEAD_DIM))
    v = v.reshape(bsz, T, H_GDN, HEAD_DIM)
    beta = jax.nn.sigmoid(b_raw.astype(jnp.float32))
    g = -jnp.exp(a_log.astype(jnp.float32)) * jax.nn.softplus(
        a_raw.astype(jnp.float32) + dt_bias.astype(jnp.float32))
    o, s_new = chunk_gated_delta(q, k, v, beta, g, s0.astype(jnp.float32))
    o = rms_norm(o, norm_w) * jax.nn.silu(z.astype(jnp.float32).reshape(bsz, T, H_GDN, HEAD_DIM))
    return o.reshape(bsz, T, W_GDN), new_buf, s_new


def retention_mixer(q, k, v, gt, pos, s0):
    bsz, T, _ = q.shape
    f = lambda t: t.astype(jnp.float32).reshape(bsz, T, H_RET, HEAD_DIM)
    q = rotary(f(q), pos)
    k = rotary(f(k), pos) * HEAD_DIM ** -0.5
    o, s_new = chunk_retention(q, k, f(v), s0.astype(jnp.float32))
    o = rms_norm(o) * jax.nn.silu(f(gt))
    return o.reshape(bsz, T, W_RET), s_new


def moba_attention(q, past_k, past_v, k_new, v_new, pos0, rel_bias):
    bsz, T, H, D = q.shape
    tk = past_k.shape[1] + T
    nb = -(-tk // MOBA_BLOCK)
    zpad = jnp.zeros((bsz, nb * MOBA_BLOCK - tk, H, D), jnp.float32)
    kb = jnp.concatenate([past_k.astype(jnp.float32), k_new.astype(jnp.float32), zpad],
                         axis=1).reshape(bsz, nb, MOBA_BLOCK, H, D)
    vb = jnp.concatenate([past_v.astype(jnp.float32), v_new.astype(jnp.float32), zpad],
                         axis=1).reshape(bsz, nb, MOBA_BLOCK, H, D)
    kmean = jnp.mean(kb, axis=2)
    n_sel = min(MOBA_TOPK, nb)
    rows = bsz * T
    n_chunks = -(-rows // MOBA_ROW_CHUNK)
    rpad = n_chunks * MOBA_ROW_CHUNK - rows
    qr = jnp.pad(q.astype(jnp.float32).reshape(rows, H, D), ((0, rpad), (0, 0), (0, 0)))
    qr = qr.reshape(n_chunks, MOBA_ROW_CHUNK, H, D)
    rb = jnp.broadcast_to(jnp.arange(bsz, dtype=jnp.int32)[:, None], (bsz, T)).reshape(-1)
    rp = jnp.broadcast_to((pos0 + jnp.arange(T, dtype=jnp.int32))[None, :], (bsz, T)).reshape(-1)
    rb = jnp.pad(rb, (0, rpad)).reshape(n_chunks, MOBA_ROW_CHUNK)
    rp = jnp.pad(rp, (0, rpad)).reshape(n_chunks, MOBA_ROW_CHUNK)
    heads = jnp.arange(H)
    offs = jnp.arange(MOBA_BLOCK, dtype=jnp.int32)
    bias_tab = rel_bias.astype(jnp.float32)

    def attend(args):
        qc, bc, pc = args
        r = qc.shape[0]
        own = pc // MOBA_BLOCK
        gate = jnp.einsum('rhd,rnhd->rhn', qc, kmean[bc])
        past = jnp.arange(nb)[None, None, :] < own[:, None, None]
        _, sel = lax.top_k(jnp.where(past, gate, NEG_INF), n_sel)
        idx = jnp.concatenate([sel, jnp.broadcast_to(own[:, None, None], (r, H, 1))], axis=-1)
        ok = jnp.concatenate([sel < own[:, None, None], jnp.ones((r, H, 1), bool)], axis=-1)
        bi = bc[:, None, None]
        hi = heads[None, :, None]
        kg = kb[bi, idx, :, hi, :]
        vg = vb[bi, idx, :, hi, :]
        dist = pc[:, None, None, None] - (idx[..., None] * MOBA_BLOCK + offs)
        mask = ok[..., None] & (dist >= 0)
        bias = bias_tab[t5_bucket(jnp.maximum(dist, 0)), heads[None, :, None, None]]
        logits = jnp.einsum('rhd,rhjkd->rhjk', qc, kg) * HEAD_DIM ** -0.5 + bias
        logits = jnp.where(mask, logits, NEG_INF)
        p = jax.nn.softmax(logits.reshape(r, H, -1), axis=-1).reshape(logits.shape)
        return jnp.einsum('rhjk,rhjkd->rhd', p, vg)

    out = lax.map(attend, (qr, rb, rp))
    return out.reshape(-1, H, D)[:rows].reshape(bsz, T, H, D)


def conv_ffn(h, buf, w_up, conv_w, w_down):
    u = h @ w_up
    ucat = jnp.concatenate([buf.astype(u.dtype), u], axis=1)
    new_buf = ucat[:, -(FFN_CONV - 1):]
    u = causal_dwconv(ucat, conv_w)
    gate, up = jnp.split(u, 2, axis=-1)
    return (jax.nn.silu(gate) * up) @ w_down, new_buf


def trunk_layer(x, c, pos0, past_k, past_v, gdn_buf, gdn_s, ret_s, ffn_buf,
                w_ada, b_ada, g_pre_mix, g_post_mix, g_pre_ffn, g_post_ffn, w_in,
                gdn_conv_w, gdn_a_log, gdn_dt_bias, gdn_norm_w, rel_bias, w_out,
                w_up, ffn_conv_w, w_down):
    bsz, T, _ = x.shape
    pos = pos0 + jnp.arange(T, dtype=jnp.int32)
    mod = jax.nn.silu(c) @ w_ada + b_ada
    sh_m, sc_m, gt_m, sh_f, sc_f, gt_f = jnp.split(mod[:, None, :], 6, axis=-1)
    h = rms_norm(x, g_pre_mix) * (1 + sc_m) + sh_m
    proj = h @ w_in
    s1 = 3 * W_GDN
    s2 = s1 + W_GDN
    s3 = s2 + H_GDN
    s4 = s3 + H_GDN
    s5 = s4 + 3 * W_MOBA
    qkv_a, z_a, a_a, b_a, qkv_b, qkvg_c = jnp.split(proj, [s1, s2, s3, s4, s5], axis=-1)
    o_a, gdn_buf_new, gdn_s_new = gdn_mixer(qkv_a, z_a, a_a, b_a, gdn_buf, gdn_s,
                                            gdn_conv_w, gdn_a_log, gdn_dt_bias, gdn_norm_w)
    qb, kb_new, vb_new = [t.reshape(bsz, T, H_MOBA, HEAD_DIM) for t in jnp.split(qkv_b, 3, axis=-1)]
    o_b = moba_attention(qb, past_k, past_v, kb_new, vb_new, pos0, rel_bias).reshape(bsz, T, W_MOBA)
    qc, kc, vc, gc = jnp.split(qkvg_c, 4, axis=-1)
    o_c, ret_s_new = retention_mixer(qc, kc, vc, gc, pos, ret_s)
    mix = jnp.concatenate([o_a, o_b.astype(o_a.dtype), o_c], axis=-1).astype(x.dtype) @ w_out
    x = x + gt_m * rms_norm(mix, g_post_mix)
    h = rms_norm(x, g_pre_ffn) * (1 + sc_f) + sh_f
    f, ffn_buf_new = conv_ffn(h, ffn_buf, w_up, ffn_conv_w, w_down)
    x = x + gt_f * rms_norm(f, g_post_ffn)
    return x, kb_new, vb_new, gdn_buf_new, gdn_s_new, ret_s_new, ffn_buf_new


def setup_inputs(seed: int = 0) -> dict:
    key = jax.random.key(seed)
    ks = jax.random.split(key, 32)
    n_pages = PAST_LEN // PAGE_SIZE
    n_used = DEC_BATCH * n_pages
    n_pool = n_used + max(1, n_used // 4)
    nrm = jax.random.normal
    f32 = jnp.float32
    dt = jnp.exp(jax.random.uniform(ks[20], (DEPTH, H_GDN), f32, math.log(1e-3), math.log(1e-1)))
    return {
        'x_prompt': nrm(ks[0], (BATCH, SEQ, D_MODEL), f32),
        'x_sample': nrm(ks[1], (DEC_BATCH, DEC_SEQ, D_MODEL), f32),
        'c_prompt': nrm(ks[2], (BATCH, D_MODEL), f32),
        'c_sample': nrm(ks[3], (DEC_BATCH, D_MODEL), f32),
        'cache_k': nrm(ks[4], (DEPTH, n_pool, PAGE_SIZE, H_MOBA, HEAD_DIM), f32),
        'cache_v': nrm(ks[5], (DEPTH, n_pool, PAGE_SIZE, H_MOBA, HEAD_DIM), f32),
        'page_table': jax.random.permutation(ks[6], n_pool)[:n_used].reshape(DEC_BATCH, n_pages).astype(jnp.int32),
        'state_gdn_conv': nrm(ks[7], (DEPTH, DEC_BATCH, GDN_CONV - 1, 3 * W_GDN), f32),
        'state_gdn': 0.5 * nrm(ks[8], (DEPTH, DEC_BATCH, H_GDN, HEAD_DIM, HEAD_DIM), f32),
        'state_ret': nrm(ks[9], (DEPTH, DEC_BATCH, H_RET, HEAD_DIM, HEAD_DIM), f32),
        'state_ffn_conv': nrm(ks[10], (DEPTH, DEC_BATCH, FFN_CONV - 1, 2 * D_FF), f32),
        'w_ada': 0.5 * D_MODEL ** -0.5 * nrm(ks[11], (DEPTH, D_MODEL, 6 * D_MODEL), f32),
        'b_ada': 0.01 * nrm(ks[12], (DEPTH, 6 * D_MODEL), f32),
        'g_pre_mix': 1.0 + 0.1 * nrm(ks[13], (DEPTH, D_MODEL), f32),
        'g_post_mix': 1.0 + 0.1 * nrm(ks[14], (DEPTH, D_MODEL), f32),
        'g_pre_ffn': 1.0 + 0.1 * nrm(ks[15], (DEPTH, D_MODEL), f32),
        'g_post_ffn': 1.0 + 0.1 * nrm(ks[16], (DEPTH, D_MODEL), f32),
        'w_in': D_MODEL ** -0.5 * nrm(ks[17], (DEPTH, D_MODEL, IN_DIM), f32),
        'gdn_conv_w': GDN_CONV ** -0.5 * nrm(ks[18], (DEPTH, GDN_CONV, 3 * W_GDN), f32),
        'gdn_a_log': jnp.log(jax.random.uniform(ks[19], (DEPTH, H_GDN), f32, 1.0, 16.0)),
        'gdn_dt_bias': dt + jnp.log(-jnp.expm1(-dt)),
        'gdn_norm_w': 1.0 + 0.1 * nrm(ks[21], (DEPTH, HEAD_DIM), f32),
        'rel_bias': 0.5 * nrm(ks[22], (NUM_BUCKETS, H_MOBA), f32),
        'w_out': MIX_WIDTH ** -0.5 * nrm(ks[23], (DEPTH, MIX_WIDTH, D_MODEL), f32),
        'w_up': D_MODEL ** -0.5 * nrm(ks[24], (DEPTH, D_MODEL, 2 * D_FF), f32),
        'ffn_conv_w': FFN_CONV ** -0.5 * nrm(ks[25], (DEPTH, FFN_CONV, 2 * D_FF), f32),
        'w_down': D_FF ** -0.5 * nrm(ks[26], (DEPTH, D_FF, D_MODEL), f32),
    }


def reference(x_prompt, x_sample, c_prompt, c_sample, cache_k, cache_v, page_table,
              state_gdn_conv, state_gdn, state_ret, state_ffn_conv,
              w_ada, b_ada, g_pre_mix, g_post_mix, g_pre_ffn, g_post_ffn, w_in,
              gdn_conv_w, gdn_a_log, gdn_dt_bias, gdn_norm_w, rel_bias, w_out,
              w_up, ffn_conv_w, w_down):
    bp = x_prompt.shape[0]
    bs = x_sample.shape[0]
    past_len = page_table.shape[1] * cache_k.shape[2]
    yp, ys = x_prompt, x_sample
    kp, vp, ksm, vsm = [], [], [], []
    gbp, gbs, gsp, gss, rsp, rss, fbp, fbs = [], [], [], [], [], [], [], []
    for l in range(DEPTH):
        lw = (w_ada[l], b_ada[l], g_pre_mix[l], g_post_mix[l], g_pre_ffn[l], g_post_ffn[l],
              w_in[l], gdn_conv_w[l], gdn_a_log[l], gdn_dt_bias[l], gdn_norm_w[l], rel_bias,
              w_out[l], w_up[l], ffn_conv_w[l], w_down[l])
        empty = jnp.zeros((bp, 0, H_MOBA, HEAD_DIM), x_prompt.dtype)
        yp, k1, v1, gb1, gs1, rs1, fb1 = trunk_layer(
            yp, c_prompt, 0, empty, empty,
            jnp.zeros((bp, GDN_CONV - 1, 3 * W_GDN), x_prompt.dtype),
            jnp.zeros((bp, H_GDN, HEAD_DIM, HEAD_DIM), jnp.float32),
            jnp.zeros((bp, H_RET, HEAD_DIM, HEAD_DIM), jnp.float32),
            jnp.zeros((bp, FFN_CONV - 1, 2 * D_FF), x_prompt.dtype), *lw)
        past_k = cache_k[l, page_table].reshape(bs, past_len, H_MOBA, HEAD_DIM)
        past_v = cache_v[l, page_table].reshape(bs, past_len, H_MOBA, HEAD_DIM)
        ys, k2, v2, gb2, gs2, rs2, fb2 = trunk_layer(
            ys, c_sample, past_len, past_k, past_v, state_gdn_conv[l], state_gdn[l],
            state_ret[l], state_ffn_conv[l], *lw)
        kp.append(k1); vp.append(v1); ksm.append(k2); vsm.append(v2)
        gbp.append(gb1); gbs.append(gb2); gsp.append(gs1); gss.append(gs2)
        rsp.append(rs1); rss.append(rs2); fbp.append(fb1); fbs.append(fb2)
    st = lambda a: jnp.stack(a, axis=0)
    return (yp, ys, st(kp), st(vp), st(ksm), st(vsm), st(gbp), st(gbs), st(gsp), st(gss),
            st(rsp), st(rss), st(fbp), st(fbs))
```

```python
import functools
import math

import numpy as np
import jax
import jax.numpy as jnp
from jax import lax
from jax.experimental import pallas as pl
from jax.experimental.pallas import tpu as pltpu

f32 = jnp.float32
bf16 = jnp.bfloat16

HEAD_DIM = 128
H_GDN = 4
H_MOBA = 8
H_RET = 4
W_GDN = H_GDN * HEAD_DIM
W_MOBA = H_MOBA * HEAD_DIM
W_RET = H_RET * HEAD_DIM
GDN_CONV = 4
FFN_CONV = 3
MOBA_BLOCK = 256
MOBA_TOPK = 3
NUM_BUCKETS = 32
MAX_DISTANCE = 128
RMS_EPS = 1e-6
NEG_INF = -1e30

SUBLANES = 8
LANES = 128
VMEM_LIMIT = 56 * 1024 * 1024

COL_QKV_A = 0
COL_Z_A = 3 * W_GDN
COL_QKV_B = COL_Z_A + W_GDN
COL_QKVG_C = COL_QKV_B + 3 * W_MOBA
COL_AB = COL_QKVG_C + 4 * W_RET
PROJ_TILE = 1280
PROJ_W = -(-(COL_AB + LANES) // PROJ_TILE) * PROJ_TILE


def _t5_bucket_starts():
    max_exact = NUM_BUCKETS // 2
    d = np.arange(0, 4 * MAX_DISTANCE)
    dd = np.maximum(d, max_exact).astype(np.float64)
    large = max_exact + (np.log(dd / max_exact) / math.log(MAX_DISTANCE / max_exact)
                         * (NUM_BUCKETS - max_exact)).astype(np.int32)
    bucket = np.where(d < max_exact, d, np.minimum(large, NUM_BUCKETS - 1))
    return [int(np.argmax(bucket >= b)) for b in range(NUM_BUCKETS)]


_BUCKET_START = _t5_bucket_starts()


def _cparams(*sem):
    return pltpu.CompilerParams(dimension_semantics=sem, vmem_limit_bytes=VMEM_LIMIT)


def _silu(x):
    return x * (1.0 / (1.0 + jnp.exp(-x)))


def _dot(a, b):
    return jnp.dot(a.astype(bf16), b.astype(bf16), preferred_element_type=f32)


def _dot_nt(a, b):
    return lax.dot_general(a.astype(bf16), b.astype(bf16), (((1,), (1,)), ((), ())),
                           preferred_element_type=f32)


def _dot_tn(a, b):
    return lax.dot_general(a.astype(bf16), b.astype(bf16), (((0,), (0,)), ((), ())),
                           preferred_element_type=f32)


def _split(a):
    hi = a.astype(bf16)
    lo = (a - hi.astype(f32)).astype(bf16)
    return hi, lo


def _dot3(a, b):
    ah, al = _split(a)
    bh, bl = _split(b)
    d = lambda x, y: jnp.dot(x, y, preferred_element_type=f32)
    return d(ah, bh) + (d(ah, bl) + d(al, bh))


def _dot3_nt(a, b):
    ah, al = _split(a)
    bh, bl = _split(b)
    d = lambda x, y: lax.dot_general(x, y, (((1,), (1,)), ((), ())), preferred_element_type=f32)
    return d(ah, bh) + (d(ah, bl) + d(al, bh))


def _ada_kernel(c_ref, w_ref, b_ref, o_ref):
    o_ref[...] = _dot(_silu(c_ref[...]), w_ref[...]) + b_ref[...]


def ada_modulation(c_all, w_ada, b_ada, tn=1024):
    depth, d_model, n = w_ada.shape
    rows = c_all.shape[0]
    return pl.pallas_call(
        _ada_kernel,
        grid=(depth, n // tn),
        in_specs=[pl.BlockSpec((rows, d_model), lambda l, j: (0, 0)),
                  pl.BlockSpec((None, d_model, tn), lambda l, j: (l, 0, j)),
                  pl.BlockSpec((None, 1, tn), lambda l, j: (l, 0, j))],
        out_specs=pl.BlockSpec((None, rows, tn), lambda l, j: (l, 0, j)),
        out_shape=jax.ShapeDtypeStruct((depth, rows, n), f32),
        compiler_params=_cparams("arbitrary", "arbitrary"),
        name="ada_modulation",
    )(c_all, w_ada, b_ada.reshape(depth, 1, n))


def _norm_mod(x, gain, scale, shift):
    ms = jnp.mean(x * x, axis=-1, keepdims=True)
    return (x * lax.rsqrt(ms + RMS_EPS) * gain) * (1.0 + scale) + shift


def _nmm_kernel(x_ref, g_ref, sc_ref, sh_ref, w_ref, o_ref, h_ref):
    @pl.when(pl.program_id(1) == 0)
    def _():
        h_ref[...] = _norm_mod(x_ref[...], g_ref[...], sc_ref[...], sh_ref[...]).astype(bf16)

    o_ref[...] = jnp.dot(h_ref[...], w_ref[...], preferred_element_type=f32)


def norm_mod_matmul(x, gain, mod, sc_blk, sh_blk, w, tm, tn):
    m, d = x.shape
    n = w.shape[1]
    per_row = mod.shape[0] != 1
    mrows = tm if per_row else 1
    mod_spec = lambda blk: pl.BlockSpec((mrows, d), (lambda i, j: (i, blk)) if per_row else (lambda i, j: (0, blk)))
    return pl.pallas_call(
        _nmm_kernel,
        grid=(m // tm, n // tn),
        in_specs=[pl.BlockSpec((tm, d), lambda i, j: (i, 0)),
                  pl.BlockSpec((1, d), lambda i, j: (0, 0)),
                  mod_spec(sc_blk), mod_spec(sh_blk),
                  pl.BlockSpec((d, tn), lambda i, j: (0, j))],
        out_specs=pl.BlockSpec((tm, tn), lambda i, j: (i, j)),
        out_shape=jax.ShapeDtypeStruct((m, n), f32),
        scratch_shapes=[pltpu.VMEM((tm, d), bf16)],
        compiler_params=_cparams("arbitrary", "arbitrary"),
        name="norm_mod_matmul",
    )(x, gain.reshape(1, d), mod, mod, w)


def _tri_inverse(lmat):
    c = lmat.shape[0]
    row = lax.broadcasted_iota(jnp.int32, (c, c), 0)
    col = lax.broadcasted_iota(jnp.int32, (c, c), 1)
    eye = jnp.where(row == col, 1.0, 0.0).astype(f32)
    diag = jnp.where(row // SUBLANES == col // SUBLANES, lmat, 0.0)
    d2 = _dot3(diag, diag)
    d4 = _dot3(d2, d2)
    t = _dot3(_dot3(eye - diag, eye + d2), eye + d4)
    b = SUBLANES
    while b < c:
        off = jnp.where((row // (2 * b) == col // (2 * b)) & (row // b != col // b), lmat, 0.0)
        t = t - _dot3(_dot3(t, off), t)
        b *= 2
    return t


def _cumsum_rows(x):
    row = lax.broadcasted_iota(jnp.int32, x.shape, 0)
    s = 1
    while s < x.shape[0]:
        x = x + jnp.where(row >= s, pltpu.roll(x, s, 0), 0.0)
        s *= 2
    return x


def _gdn_kernel(qkv_ref, z_ref, ab_ref, hist_ref, s0_ref, cw_ref, alog_ref, dtb_ref, nw_ref,
                o_ref, sfin_ref, s_scr, carry, *, n_valid):
    c_idx = pl.program_id(1)
    c = qkv_ref.shape[0]

    @pl.when(c_idx == 0)
    def _():
        s_scr[...] = s0_ref[...]
        carry[...] = hist_ref[...]

    raw = qkv_ref[...]
    ext = jnp.concatenate([carry[...], raw], axis=0)
    carry[...] = raw[c - SUBLANES:, :]
    conv = ext[SUBLANES:] * cw_ref[GDN_CONV - 1:GDN_CONV, :]
    for i in range(1, GDN_CONV):
        conv = conv + pltpu.roll(ext, i, 0)[SUBLANES:] * cw_ref[GDN_CONV - 1 - i:GDN_CONV - i, :]
    qkv = _silu(conv)

    rowid = lax.broadcasted_iota(jnp.int32, (c, LANES), 0)
    valid = rowid < n_valid
    ab = ab_ref[...]
    sp = jnp.maximum(ab + dtb_ref[...], 0.0) + jnp.log1p(jnp.exp(-jnp.abs(ab + dtb_ref[...])))
    g_all = jnp.where(valid, -jnp.exp(alog_ref[...]) * sp, 0.0)
    beta_all = jnp.where(valid, 1.0 / (1.0 + jnp.exp(-ab)), 0.0)
    gc_all = _cumsum_rows(g_all)
    gc_t = gc_all.T

    r2 = lax.broadcasted_iota(jnp.int32, (c, c), 0)
    c2 = lax.broadcasted_iota(jnp.int32, (c, c), 1)
    lower = r2 >= c2
    strict = r2 > c2
    nw = nw_ref[...]

    for h in range(H_GDN):
        sl = slice(h * HEAD_DIM, (h + 1) * HEAD_DIM)
        q = qkv[:, sl]
        k = qkv[:, W_GDN + h * HEAD_DIM:W_GDN + (h + 1) * HEAD_DIM]
        v = qkv[:, 2 * W_GDN + h * HEAD_DIM:2 * W_GDN + (h + 1) * HEAD_DIM]
        q = q * lax.rsqrt(jnp.sum(q * q, axis=-1, keepdims=True) + 1e-6) * HEAD_DIM ** -0.5
        k = k * lax.rsqrt(jnp.sum(k * k, axis=-1, keepdims=True) + 1e-6)
        beta = beta_all[:, H_GDN + h:H_GDN + h + 1]
        gc = gc_all[:, h:h + 1]
        gc_row = gc_t[h:h + 1, :]
        g_last = gc_all[c - 1:c, h:h + 1]
        decay = jnp.where(lower, jnp.exp(jnp.where(lower, gc - gc_row, 0.0)), 0.0)
        kb = k * beta
        lmat = jnp.where(strict, _dot_nt(kb, k) * decay, 0.0)
        tinv = _tri_inverse(lmat)
        rhs = jnp.concatenate([v * beta, kb * jnp.exp(gc)], axis=1)
        uw = _dot(tinv, rhs)
        u, w = uw[:, :HEAD_DIM], uw[:, HEAD_DIM:]
        qk = _dot_nt(q, k) * decay
        s = s_scr[h]
        v_new = u - _dot(w, s)
        o = _dot(q * jnp.exp(gc), s) + _dot(qk, v_new)
        s_scr[h] = s * jnp.exp(g_last) + _dot_tn(k * jnp.exp(g_last - gc), v_new)
        o = o * lax.rsqrt(jnp.mean(o * o, axis=-1, keepdims=True) + RMS_EPS) * nw
        o_ref[:, sl] = o * _silu(z_ref[:, sl])

    @pl.when(c_idx == pl.num_programs(1) - 1)
    def _():
        sfin_ref[...] = s_scr[...]


def gdn_mixer(proj, hist, s0, conv_w, a_log, dt_bias, norm_w, chunk, n_valid):
    s_n, t, _ = proj.shape
    n_chunks = t // chunk
    lane_pad = lambda a: jnp.zeros((1, LANES), f32).at[0, :H_GDN].set(a)
    return pl.pallas_call(
        functools.partial(_gdn_kernel, n_valid=n_valid),
        grid=(s_n, n_chunks),
        in_specs=[pl.BlockSpec((None, chunk, 3 * W_GDN), lambda s, c: (s, c, COL_QKV_A // (3 * W_GDN))),
                  pl.BlockSpec((None, chunk, W_GDN), lambda s, c: (s, c, COL_Z_A // W_GDN)),
                  pl.BlockSpec((None, chunk, LANES), lambda s, c: (s, c, COL_AB // LANES)),
                  pl.BlockSpec((None, SUBLANES, 3 * W_GDN), lambda s, c: (s, 0, 0)),
                  pl.BlockSpec((None, H_GDN, HEAD_DIM, HEAD_DIM), lambda s, c: (s, 0, 0, 0)),
                  pl.BlockSpec((GDN_CONV, 3 * W_GDN), lambda s, c: (0, 0)),
                  pl.BlockSpec((1, LANES), lambda s, c: (0, 0)),
                  pl.BlockSpec((1, LANES), lambda s, c: (0, 0)),
                  pl.BlockSpec((1, HEAD_DIM), lambda s, c: (0, 0))],
        out_specs=[pl.BlockSpec((None, chunk, W_GDN), lambda s, c: (s, c, 0)),
                   pl.BlockSpec((None, H_GDN, HEAD_DIM, HEAD_DIM), lambda s, c: (s, 0, 0, 0))],
        out_shape=[jax.ShapeDtypeStruct((s_n, t, W_GDN), f32),
                   jax.ShapeDtypeStruct((s_n, H_GDN, HEAD_DIM, HEAD_DIM), f32)],
        scratch_shapes=[pltpu.VMEM((H_GDN, HEAD_DIM, HEAD_DIM), f32),
                        pltpu.VMEM((SUBLANES, 3 * W_GDN), f32)],
        compiler_params=_cparams("arbitrary", "arbitrary"),
        name="gdn_mixer",
    )(proj, proj, proj, hist, s0, conv_w, lane_pad(a_log), lane_pad(dt_bias), norm_w.reshape(1, HEAD_DIM))


def _ret_kernel(q_ref, k_ref, v_ref, g_ref, cs_ref, sn_ref, s0_ref, o_ref, sfin_ref, s_scr, *, n_valid):
    c_idx = pl.program_id(1)
    c = q_ref.shape[0]
    c_len = min(c, n_valid)

    @pl.when(c_idx == 0)
    def _():
        s_scr[...] = s0_ref[...]

    cs = cs_ref[...]
    sn = sn_ref[...]
    rot = lambda x: x * cs + pltpu.roll(x, HEAD_DIM // 2, 1) * sn
    r2 = lax.broadcasted_iota(jnp.int32, (c, c), 0)
    c2 = lax.broadcasted_iota(jnp.int32, (c, c), 1)
    rel = (r2 - c2).astype(f32)
    idx = lax.broadcasted_iota(jnp.int32, (c, 1), 0)
    idx_f = idx.astype(f32)
    valid = idx < n_valid

    for h in range(H_RET):
        lg = math.log(1.0 - 2.0 ** (-5.0 - h))
        sl = slice(h * HEAD_DIM, (h + 1) * HEAD_DIM)
        q = rot(q_ref[:, sl])
        k = rot(k_ref[:, sl]) * HEAD_DIM ** -0.5
        v = v_ref[:, sl]
        gt = g_ref[:, sl]
        dmat = jnp.where(rel >= 0, jnp.exp(jnp.maximum(rel, 0.0) * lg), 0.0)
        xi = jnp.exp((idx_f + 1.0) * lg)
        zeta = jnp.where(valid, jnp.exp((c_len - 1.0 - idx_f) * lg), 0.0)
        s = s_scr[h]
        o = _dot(_dot_nt(q, k) * dmat, v) + _dot(q * xi, s)
        s_scr[h] = math.exp(c_len * lg) * s + _dot_tn(k * zeta, v)
        o = o * lax.rsqrt(jnp.mean(o * o, axis=-1, keepdims=True) + RMS_EPS)
        o_ref[:, sl] = o * _silu(gt)

    @pl.when(c_idx == pl.num_programs(1) - 1)
    def _():
        sfin_ref[...] = s_scr[...]


def retention_mixer(proj, cs, sn, s0, chunk, n_valid):
    s_n, t, _ = proj.shape
    n_chunks = t // chunk
    col_spec = lambda part: pl.BlockSpec((None, chunk, W_RET), lambda s, c: (s, c, COL_QKVG_C // W_RET + part))
    return pl.pallas_call(
        functools.partial(_ret_kernel, n_valid=n_valid),
        grid=(s_n, n_chunks),
        in_specs=[col_spec(0), col_spec(1), col_spec(2), col_spec(3),
                  pl.BlockSpec((chunk, HEAD_DIM), lambda s, c: (c, 0)),
                  pl.BlockSpec((chunk, HEAD_DIM), lambda s, c: (c, 0)),
                  pl.BlockSpec((None, H_RET, HEAD_DIM, HEAD_DIM), lambda s, c: (s, 0, 0, 0))],
        out_specs=[pl.BlockSpec((None, chunk, W_RET), lambda s, c: (s, c, 0)),
                   pl.BlockSpec((None, H_RET, HEAD_DIM, HEAD_DIM), lambda s, c: (s, 0, 0, 0))],
        out_shape=[jax.ShapeDtypeStruct((s_n, t, W_RET), f32),
                   jax.ShapeDtypeStruct((s_n, H_RET, HEAD_DIM, HEAD_DIM), f32)],
        scratch_shapes=[pltpu.VMEM((H_RET, HEAD_DIM, HEAD_DIM), f32)],
        compiler_params=_cparams("arbitrary", "arbitrary"),
        name="retention_mixer",
    )(proj, proj, proj, proj, cs, sn, s0)


def _rotary_tables(pos):
    half = HEAD_DIM // 2
    inv = 1.0 / (10000.0 ** jnp.linspace(0.0, 1.0, half, dtype=f32))
    ang = pos.astype(f32)[:, None] * inv[None, :]
    cos, sin = jnp.cos(ang), jnp.sin(ang)
    return jnp.concatenate([cos, cos], axis=1), jnp.concatenate([-sin, sin], axis=1)


def _t5_bias(dist, tab_ref, h):
    val = jnp.full(dist.shape, tab_ref[0, h], f32)
    for b in range(1, NUM_BUCKETS):
        val = jnp.where(dist >= _BUCKET_START[b], tab_ref[b, h], val)
    return val


def _top_blocks(gate, n_sel):
    lane = lax.broadcasted_iota(jnp.int32, gate.shape, 1)
    sel = jnp.zeros(gate.shape, f32)
    for _ in range(n_sel):
        m = jnp.max(gate, axis=1, keepdims=True)
        first = jnp.min(jnp.where(gate == m, lane, gate.shape[1]), axis=1, keepdims=True)
        pick = (lane == first) & (m > 0.5 * NEG_INF)
        sel = jnp.where(pick, 1.0, sel)
        gate = jnp.where(pick, NEG_INF, gate)
    return sel


def _moba_prompt_kernel(tab_ref, q_ref, k_ref, v_ref, o_ref, kmean_scr, bias_scr, sel_scr, m_scr, l_scr, acc_scr):
    h = pl.program_id(0)
    i = pl.program_id(1)
    blk = MOBA_BLOCK
    n_blocks = k_ref.shape[0] // blk
    scale = HEAD_DIM ** -0.5
    r2 = lax.broadcasted_iota(jnp.int32, (blk, blk), 0)
    c2 = lax.broadcasted_iota(jnp.int32, (blk, blk), 1)

    @pl.when(i == 0)
    def _():
        km = jnp.mean(k_ref[...].reshape(n_blocks, blk, HEAD_DIM), axis=1)
        kmean_scr[...] = jnp.zeros(kmean_scr.shape, f32)
        kmean_scr[0:n_blocks, :] = km
        bias_scr[0] = _t5_bias(r2 - c2, tab_ref, h)
        bias_scr[1] = _t5_bias(r2 - c2 + blk, tab_ref, h)

    q = q_ref[...]
    lane = lax.broadcasted_iota(jnp.int32, (blk, LANES), 1)
    gate = jnp.where(lane < i, _dot3_nt(q, kmean_scr[...]), NEG_INF)
    sel_scr[...] = _top_blocks(gate, MOBA_TOPK)

    def attend(kj, vj, bias, mask, first):
        s = _dot_nt(q, kj) * scale + bias
        s = jnp.where(mask, s, NEG_INF)
        m_old = jnp.full((blk, 1), NEG_INF, f32) if first else m_scr[...]
        m_new = jnp.maximum(m_old, jnp.max(s, axis=1, keepdims=True))
        p = jnp.where(mask, jnp.exp(s - m_new), 0.0)
        pv = _dot(p, vj)
        if first:
            l_scr[...] = jnp.sum(p, axis=1, keepdims=True)
            acc_scr[...] = pv
        else:
            alpha = jnp.exp(m_old - m_new)
            l_scr[...] = alpha * l_scr[...] + jnp.sum(p, axis=1, keepdims=True)
            acc_scr[...] = alpha * acc_scr[...] + pv
        m_scr[...] = m_new

    own = pl.ds(pl.multiple_of(i * blk, blk), blk)
    attend(k_ref[own, :], v_ref[own, :], bias_scr[0], r2 >= c2, True)

    far_bias = tab_ref[NUM_BUCKETS - 1, h]

    def selected(j):
        col = jnp.sum(jnp.where(lane == j, sel_scr[...], 0.0), axis=1, keepdims=True)
        return jnp.broadcast_to(col > 0.5, (blk, blk))

    @pl.when(i >= 1)
    def _():
        prev = pl.ds(pl.multiple_of((i - 1) * blk, blk), blk)
        attend(k_ref[prev, :], v_ref[prev, :], bias_scr[1], selected(i - 1), False)

    def body(j, carry):
        rows = pl.ds(pl.multiple_of(j * blk, blk), blk)
        attend(k_ref[rows, :], v_ref[rows, :], far_bias, selected(j), False)
        return carry

    lax.fori_loop(0, jnp.maximum(i - 1, 0), body, 0)
    o_ref[...] = acc_scr[...] / l_scr[...]


def moba_prompt(proj, rel_bias):
    t = proj.shape[0]
    blk = MOBA_BLOCK
    assert t % blk == 0 and t // blk <= LANES and 2 * blk >= MAX_DISTANCE
    qb, kb, vb = (COL_QKV_B // HEAD_DIM, COL_QKV_B // HEAD_DIM + H_MOBA, COL_QKV_B // HEAD_DIM + 2 * H_MOBA)
    return pl.pallas_call(
        _moba_prompt_kernel,
        grid=(H_MOBA, t // blk),
        in_specs=[pl.BlockSpec(memory_space=pltpu.SMEM),
                  pl.BlockSpec((blk, HEAD_DIM), lambda h, i: (i, qb + h)),
                  pl.BlockSpec((t, HEAD_DIM), lambda h, i: (0, kb + h)),
                  pl.BlockSpec((t, HEAD_DIM), lambda h, i: (0, vb + h))],
        out_specs=pl.BlockSpec((blk, HEAD_DIM), lambda h, i: (i, h)),
        out_shape=jax.ShapeDtypeStruct((t, W_MOBA), f32),
        scratch_shapes=[pltpu.VMEM((LANES, HEAD_DIM), f32),
                        pltpu.VMEM((2, blk, blk), f32),
                        pltpu.VMEM((blk, LANES), f32),
                        pltpu.VMEM((blk, 1), f32),
                        pltpu.VMEM((blk, 1), f32),
                        pltpu.VMEM((blk, HEAD_DIM), f32)],
        compiler_params=_cparams("arbitrary", "arbitrary"),
        name="moba_prompt",
    )(rel_bias, proj, proj, proj)


def _moba_sample_kernel(pt_ref, tab_ref, q_ref, kn_ref, vn_ref, *rest, n_pages, n_new):
    kp = rest[:n_pages]
    vp = rest[n_pages:2 * n_pages]
    o_ref = rest[2 * n_pages]
    page = kp[0].shape[0]
    blk = MOBA_BLOCK
    past = n_pages * page
    n_blocks = past // blk
    rows = q_ref.shape[0]
    scale = HEAD_DIM ** -0.5

    t_p = lax.broadcasted_iota(jnp.int32, (rows, past), 0)
    c_p = lax.broadcasted_iota(jnp.int32, (rows, past), 1)
    dist_p = past + t_p - c_p
    blk_p = c_p // blk
    t_o = lax.broadcasted_iota(jnp.int32, (rows, rows), 0)
    c_o = lax.broadcasted_iota(jnp.int32, (rows, rows), 1)
    own_mask = (c_o <= t_o) & (c_o < n_new)
    lane_b = lax.broadcasted_iota(jnp.int32, (rows, LANES), 1)

    for h in range(H_MOBA):
        sl = slice(h * HEAD_DIM, (h + 1) * HEAD_DIM)
        q = q_ref[:, sl]
        k_new = kn_ref[:, sl]
        v_new = vn_ref[:, sl]
        k_h = jnp.concatenate([r[:, h, :] for r in kp], axis=0)
        v_h = jnp.concatenate([r[:, h, :] for r in vp], axis=0)
        kmean = jnp.mean(k_h.reshape(n_blocks, blk, HEAD_DIM), axis=1)
        kmean = jnp.concatenate([kmean, jnp.zeros((LANES - n_blocks, HEAD_DIM), f32)], axis=0)
        gate = jnp.where(lane_b < n_blocks, _dot3_nt(q, kmean), NEG_INF)
        sel = _top_blocks(gate, min(MOBA_TOPK, n_blocks))
        mask_p = jnp.zeros((rows, past), jnp.bool_)
        for n in range(n_blocks):
            mask_p = mask_p | ((blk_p == n) & (sel[:, n:n + 1] > 0.5))
        s_p = _dot_nt(q, k_h) * scale + _t5_bias(dist_p, tab_ref, h)
        s_p = jnp.where(mask_p, s_p, NEG_INF)
        s_o = _dot_nt(q, k_new) * scale + _t5_bias(t_o - c_o, tab_ref, h)
        s_o = jnp.where(own_mask, s_o, NEG_INF)
        m = jnp.maximum(jnp.max(s_p, axis=1, keepdims=True), jnp.max(s_o, axis=1, keepdims=True))
        p_p = jnp.where(mask_p, jnp.exp(s_p - m), 0.0)
        p_o = jnp.where(own_mask, jnp.exp(s_o - m), 0.0)
        denom = jnp.sum(p_p, axis=1, keepdims=True) + jnp.sum(p_o, axis=1, keepdims=True)
        o_ref[:, sl] = (_dot(p_p, v_h) + _dot(p_o, v_new)) / denom


def moba_sample(proj, cache_k, cache_v, page_table, rel_bias, layer, n_new):
    bsz, rows, _ = proj.shape
    n_pages = page_table.shape[1]
    page = cache_k.shape[2]
    assert (n_pages * page) % MOBA_BLOCK == 0 and MOBA_BLOCK % page == 0 and n_new <= rows
    page_spec = lambda p: pl.BlockSpec((None, None, page, H_MOBA, HEAD_DIM),
                                       lambda b, pt: (layer, pt[b, p], 0, 0, 0))
    grid_spec = pltpu.PrefetchScalarGridSpec(
        num_scalar_prefetch=1,
        grid=(bsz,),
        in_specs=([pl.BlockSpec(memory_space=pltpu.SMEM)]
                  + [pl.BlockSpec((None, rows, W_MOBA), (lambda b, pt, part=part: (b, 0, COL_QKV_B // W_MOBA + part)))
                     for part in range(3)]
                  + [page_spec(p) for p in range(n_pages)] * 2),
        out_specs=pl.BlockSpec((None, rows, W_MOBA), lambda b, pt: (b, 0, 0)),
    )
    return pl.pallas_call(
        functools.partial(_moba_sample_kernel, n_pages=n_pages, n_new=n_new),
        grid_spec=grid_spec,
        out_shape=jax.ShapeDtypeStruct((bsz, rows, W_MOBA), f32),
        compiler_params=_cparams("arbitrary"),
        name="moba_sample",
    )(page_table, rel_bias, proj, proj, proj, *([cache_k] * n_pages), *([cache_v] * n_pages))


def _post(x, y, gain, gate):
    return x + gate * (y * lax.rsqrt(jnp.mean(y * y, axis=-1, keepdims=True) + RMS_EPS) * gain)


def _out_proj_kernel(oa_ref, ob_ref, oc_ref, w_ref, x_ref, gt_ref, g_ref, o_ref):
    y = (jnp.dot(oa_ref[...].astype(bf16), w_ref[0:W_GDN, :], preferred_element_type=f32)
         + jnp.dot(ob_ref[...].astype(bf16), w_ref[W_GDN:W_GDN + W_MOBA, :], preferred_element_type=f32)
         + jnp.dot(oc_ref[...].astype(bf16), w_ref[W_GDN + W_MOBA:, :], preferred_element_type=f32))
    o_ref[...] = _post(x_ref[...], y, g_ref[...], gt_ref[...])


def out_proj_residual(o_a, o_b, o_c, w_out, x, mod, gt_blk, gain, tm):
    m, d = x.shape
    per_row = mod.shape[0] != 1
    mod_spec = pl.BlockSpec((tm if per_row else 1, d), (lambda i: (i, gt_blk)) if per_row else (lambda i: (0, gt_blk)))
    return pl.pallas_call(
        _out_proj_kernel,
        grid=(m // tm,),
        in_specs=[pl.BlockSpec((tm, W_GDN), lambda i: (i, 0)),
                  pl.BlockSpec((tm, W_MOBA), lambda i: (i, 0)),
                  pl.BlockSpec((tm, W_RET), lambda i: (i, 0)),
                  pl.BlockSpec(w_out.shape, lambda i: (0, 0)),
                  pl.BlockSpec((tm, d), lambda i: (i, 0)),
                  mod_spec,
                  pl.BlockSpec((1, d), lambda i: (0, 0))],
        out_specs=pl.BlockSpec((tm, d), lambda i: (i, 0)),
        out_shape=jax.ShapeDtypeStruct((m, d), f32),
        compiler_params=_cparams("arbitrary"),
        name="out_proj_residual",
    )(o_a, o_b, o_c, w_out, x, mod, gain.reshape(1, d))


def _ffn_down_kernel(ug_ref, uu_ref, pg_ref, pu_ref, hg_ref, hu_ref, cwg_ref, cwu_ref, wd_ref,
                     x_ref, gt_ref, g_ref, o_ref, acc_ref, *, row_shift):
    i = pl.program_id(0)
    f = pl.program_id(1)
    hdr = hg_ref.shape[0]
    first = i == 0

    def conv(u_ref, p_ref, h_ref, cw_ref):
        head = jnp.where(first, h_ref[...], p_ref[...])
        ext = jnp.concatenate([head, u_ref[...]], axis=0)
        out = ext[hdr:] * cw_ref[FFN_CONV - 1:FFN_CONV, :]
        for j in range(1, FFN_CONV):
            out = out + pltpu.roll(ext, j * row_shift, 0)[hdr:] * cw_ref[FFN_CONV - 1 - j:FFN_CONV - j, :]
        return out

    act = _silu(conv(ug_ref, pg_ref, hg_ref, cwg_ref)) * conv(uu_ref, pu_ref, hu_ref, cwu_ref)
    part = jnp.dot(act.astype(bf16), wd_ref[...], preferred_element_type=f32)

    @pl.when(f == 0)
    def _():
        acc_ref[...] = part

    @pl.when(f > 0)
    def _():
        acc_ref[...] += part

    @pl.when(f == pl.num_programs(1) - 1)
    def _():
        o_ref[...] = _post(x_ref[...], acc_ref[...], g_ref[...], gt_ref[...])


def ffn_down_residual(u, hist, conv_w, w_down, x, mod, gt_blk, gain, tm, tf, row_shift):
    m, d = x.shape
    ff = w_down.shape[0]
    nf = ff // tf
    hdr = hist.shape[0]
    assert hdr >= (FFN_CONV - 1) * row_shift and tm % hdr == 0
    per_row = mod.shape[0] != 1
    mod_spec = pl.BlockSpec((tm if per_row else 1, d),
                            (lambda i, f: (i, gt_blk)) if per_row else (lambda i, f: (0, gt_blk)))
    prev = lambda off: (lambda i, f: (jnp.maximum(i * (tm // hdr) - 1, 0), f + off))
    return pl.pallas_call(
        functools.partial(_ffn_down_kernel, row_shift=row_shift),
        grid=(m // tm, nf),
        in_specs=[pl.BlockSpec((tm, tf), lambda i, f: (i, f)),
                  pl.BlockSpec((tm, tf), lambda i, f: (i, f + nf)),
                  pl.BlockSpec((hdr, tf), prev(0)),
                  pl.BlockSpec((hdr, tf), prev(nf)),
                  pl.BlockSpec((hdr, tf), lambda i, f: (0, f)),
                  pl.BlockSpec((hdr, tf), lambda i, f: (0, f + nf)),
                  pl.BlockSpec((FFN_CONV, tf), lambda i, f: (0, f)),
                  pl.BlockSpec((FFN_CONV, tf), lambda i, f: (0, f + nf)),
                  pl.BlockSpec((tf, d), lambda i, f: (f, 0)),
                  pl.BlockSpec((tm, d), lambda i, f: (i, 0)),
                  mod_spec,
                  pl.BlockSpec((1, d), lambda i, f: (0, 0))],
        out_specs=pl.BlockSpec((tm, d), lambda i, f: (i, 0)),
        out_shape=jax.ShapeDtypeStruct((m, d), f32),
        scratch_shapes=[pltpu.VMEM((tm, d), f32)],
        compiler_params=_cparams("arbitrary", "arbitrary"),
        name="ffn_down_residual",
    )(u, u, u, u, hist, hist, conv_w, conv_w, w_down, x, mod, gain.reshape(1, d))


def _permute_w_in(w_in):
    s1 = 3 * W_GDN
    s2 = s1 + W_GDN
    s4 = s2 + 2 * H_GDN
    ab = w_in[..., s2:s4]
    pad = jnp.zeros(w_in.shape[:-1] + (PROJ_W - COL_AB - 2 * H_GDN,), w_in.dtype)
    return jnp.concatenate([w_in[..., :s2], w_in[..., s4:], ab, pad], axis=-1).astype(bf16)


def kernel(x_prompt, x_sample, c_prompt, c_sample, cache_k, cache_v, page_table, state_gdn_conv, state_gdn, state_ret, state_ffn_conv, w_ada, b_ada, g_pre_mix, g_post_mix, g_pre_ffn, g_post_ffn, w_in, gdn_conv_w, gdn_a_log, gdn_dt_bias, gdn_norm_w, rel_bias, w_out, w_up, ffn_conv_w, w_down):
    bp, seq, d = x_prompt.shape
    bs, dec = x_sample.shape[:2]
    depth = w_in.shape[0]
    assert bp == 1 and dec <= SUBLANES
    past_len = page_table.shape[1] * cache_k.shape[2]
    d_ff = w_down.shape[1]
    pad_t = SUBLANES - dec

    w_in_b = _permute_w_in(w_in)
    w_out_b = w_out.astype(bf16)
    w_up_b = w_up.astype(bf16)
    w_down_b = w_down.astype(bf16)

    n_c = bp + bs
    c_all = jnp.concatenate([c_prompt, c_sample, jnp.zeros((-n_c % SUBLANES, d), f32)], axis=0)
    mod = ada_modulation(c_all, w_ada, b_ada)

    cs_p, sn_p = _rotary_tables(jnp.arange(seq, dtype=jnp.int32))
    cs_s, sn_s = _rotary_tables(past_len + jnp.arange(SUBLANES, dtype=jnp.int32))

    yp = x_prompt.reshape(seq, d)
    ys = jnp.swapaxes(x_sample, 0, 1).reshape(dec * bs, d)
    to_bm = lambda a: jnp.pad(jnp.swapaxes(a.reshape(dec, bs, -1), 0, 1), ((0, 0), (0, pad_t), (0, 0)))
    to_tm = lambda a: jnp.swapaxes(a[:, :dec], 0, 1).reshape(dec * bs, -1)

    outs = [[] for _ in range(12)]
    for l in range(depth):
        mod_p = mod[l, 0:bp]
        mod_s = jnp.tile(mod[l, bp:bp + bs], (dec, 1))

        proj_p = norm_mod_matmul(yp, g_pre_mix[l], mod_p, 1, 0, w_in_b[l], tm=1024, tn=PROJ_TILE)
        proj_p3 = proj_p.reshape(1, seq, PROJ_W)
        oa_p, gs_p = gdn_mixer(proj_p3, jnp.zeros((1, SUBLANES, 3 * W_GDN), f32),
                               jnp.zeros((1, H_GDN, HEAD_DIM, HEAD_DIM), f32),
                               gdn_conv_w[l], gdn_a_log[l], gdn_dt_bias[l], gdn_norm_w[l], chunk=64, n_valid=64)
        oc_p, rs_p = retention_mixer(proj_p3, cs_p, sn_p, jnp.zeros((1, H_RET, HEAD_DIM, HEAD_DIM), f32),
                                     chunk=64, n_valid=64)
        ob_p = moba_prompt(proj_p, rel_bias)
        yp = out_proj_residual(oa_p[0], ob_p, oc_p[0], w_out_b[l], yp, mod_p, 2, g_post_mix[l], tm=512)

        proj_s = norm_mod_matmul(ys, g_pre_mix[l], mod_s, 1, 0, w_in_b[l], tm=dec * bs, tn=PROJ_TILE)
        proj_sb = to_bm(proj_s)
        hist_s = jnp.pad(state_gdn_conv[l], ((0, 0), (SUBLANES - (GDN_CONV - 1), 0), (0, 0)))
        oa_s, gs_s = gdn_mixer(proj_sb, hist_s, state_gdn[l], gdn_conv_w[l], gdn_a_log[l], gdn_dt_bias[l],
                               gdn_norm_w[l], chunk=SUBLANES, n_valid=dec)
        oc_s, rs_s = retention_mixer(proj_sb, cs_s, sn_s, state_ret[l], chunk=SUBLANES, n_valid=dec)
        ob_s = moba_sample(proj_sb, cache_k, cache_v, page_table, rel_bias, l, dec)
        ys = out_proj_residual(to_tm(oa_s), to_tm(ob_s), to_tm(oc_s), w_out_b[l], ys, mod_s, 2,
                               g_post_mix[l], tm=dec * bs)

        u_p = norm_mod_matmul(yp, g_pre_ffn[l], mod_p, 4, 3, w_up_b[l], tm=1024, tn=1024)
        yp = ffn_down_residual(u_p, jnp.zeros((SUBLANES, 2 * d_ff), f32), ffn_conv_w[l], w_down_b[l], yp,
                               mod_p, 5, g_post_ffn[l], tm=512, tf=512, row_shift=1)
        u_s = norm_mod_matmul(ys, g_pre_ffn[l], mod_s, 4, 3, w_up_b[l], tm=dec * bs, tn=1024)
        hist_f = jnp.swapaxes(state_ffn_conv[l], 0, 1).reshape((FFN_CONV - 1) * bs, 2 * d_ff)
        ys = ffn_down_residual(u_s, hist_f, ffn_conv_w[l], w_down_b[l], ys, mod_s, 5, g_post_ffn[l],
                               tm=dec * bs, tf=512, row_shift=bs)

        kcol = slice(COL_QKV_B + W_MOBA, COL_QKV_B + 2 * W_MOBA)
        vcol = slice(COL_QKV_B + 2 * W_MOBA, COL_QKV_B + 3 * W_MOBA)
        outs[0].append(proj_p[:, kcol].reshape(bp, seq, H_MOBA, HEAD_DIM))
        outs[1].append(proj_p[:, vcol].reshape(bp, seq, H_MOBA, HEAD_DIM))
        outs[2].append(proj_sb[:, :dec, kcol].reshape(bs, dec, H_MOBA, HEAD_DIM))
        outs[3].append(proj_sb[:, :dec, vcol].reshape(bs, dec, H_MOBA, HEAD_DIM))
        outs[4].append(proj_p[seq - (GDN_CONV - 1):, :3 * W_GDN].reshape(bp, GDN_CONV - 1, 3 * W_GDN))
        outs[5].append(proj_sb[:, dec - (GDN_CONV - 1):dec, :3 * W_GDN])
        outs[6].append(gs_p)
        outs[7].append(gs_s)
        outs[8].append(rs_p)
        outs[9].append(rs_s)
        outs[10].append(u_p[seq - (FFN_CONV - 1):].reshape(bp, FFN_CONV - 1, 2 * d_ff))
        outs[11].append(jnp.swapaxes(u_s.reshape(dec, bs, 2 * d_ff)[dec - (FFN_CONV - 1):], 0, 1))

    y_prompt = yp.reshape(bp, seq, d)
    y_sample = jnp.swapaxes(ys.reshape(dec, bs, d), 0, 1)
    return (y_prompt, y_sample) + tuple(jnp.stack(o, axis=0) for o in outs)
```

```python
import functools
import math

import numpy as np
import jax
import jax.numpy as jnp
from jax import lax
from jax.experimental import pallas as pl
from jax.experimental.pallas import tpu as pltpu

f32 = jnp.float32
bf16 = jnp.bfloat16

HEAD_DIM = 128
H_GDN = 4
H_MOBA = 8
H_RET = 4
W_GDN = H_GDN * HEAD_DIM
W_MOBA = H_MOBA * HEAD_DIM
W_RET = H_RET * HEAD_DIM
GDN_CONV = 4
FFN_CONV = 3
MOBA_BLOCK = 256
MOBA_TOPK = 3
NUM_BUCKETS = 32
MAX_DISTANCE = 128
RMS_EPS = 1e-6
NEG_INF = -1e30

SUBLANES = 8
LANES = 128
MXU_DIM = 256
VMEM_LIMIT = 56 * 1024 * 1024

COL_QKV_A = 0
COL_Z_A = 3 * W_GDN
COL_QKV_B = COL_Z_A + W_GDN
COL_QKVG_C = COL_QKV_B + 3 * W_MOBA
COL_AB = COL_QKVG_C + 4 * W_RET
PROJ_TILE = 5 * MXU_DIM
PROJ_W = -(-(COL_AB + LANES) // PROJ_TILE) * PROJ_TILE

GDN_CHUNK = 64
RET_CHUNK = 64
FAR_CHUNK = 4


def _t5_bucket_starts():
    max_exact = NUM_BUCKETS // 2
    d = np.arange(0, 4 * MAX_DISTANCE)
    dd = np.maximum(d, max_exact).astype(np.float64)
    large = max_exact + (np.log(dd / max_exact) / math.log(MAX_DISTANCE / max_exact)
                         * (NUM_BUCKETS - max_exact)).astype(np.int32)
    bucket = np.where(d < max_exact, d, np.minimum(large, NUM_BUCKETS - 1))
    return [int(np.argmax(bucket >= b)) for b in range(NUM_BUCKETS)]


_BUCKET_START = _t5_bucket_starts()


def _cparams(*sem):
    return pltpu.CompilerParams(dimension_semantics=sem, vmem_limit_bytes=VMEM_LIMIT)


def _silu(x):
    return x * (1.0 / (1.0 + jnp.exp(-x)))


def _dot(a, b):
    return jnp.dot(a.astype(bf16), b.astype(bf16), preferred_element_type=f32)


def _dot_nt(a, b):
    return lax.dot_general(a.astype(bf16), b.astype(bf16), (((1,), (1,)), ((), ())),
                           preferred_element_type=f32)


def _dot_tn(a, b):
    return lax.dot_general(a.astype(bf16), b.astype(bf16), (((0,), (0,)), ((), ())),
                           preferred_element_type=f32)


def _split(a):
    hi = a.astype(bf16)
    lo = (a - hi.astype(f32)).astype(bf16)
    return hi, lo


def _dot3(a, b):
    ah, al = _split(a)
    bh, bl = _split(b)
    d = lambda x, y: jnp.dot(x, y, preferred_element_type=f32)
    return d(ah, bh) + (d(ah, bl) + d(al, bh))


def _dot3_nt(a, b):
    ah, al = _split(a)
    bh, bl = _split(b)
    d = lambda x, y: lax.dot_general(x, y, (((1,), (1,)), ((), ())), preferred_element_type=f32)
    return d(ah, bh) + (d(ah, bl) + d(al, bh))


def _fold(x, op):
    return functools.reduce(op, [x[:, i * LANES:(i + 1) * LANES] for i in range(x.shape[1] // LANES)])


def _ada_kernel(c_ref, w_ref, b_ref, o_ref):
    o_ref[...] = _dot(_silu(c_ref[...]), w_ref[...]) + b_ref[...]


def ada_modulation(c_all, w_ada, b_ada, tn=1024):
    depth, d_model, n = w_ada.shape
    rows = c_all.shape[0]
    return pl.pallas_call(
        _ada_kernel,
        grid=(depth, n // tn),
        in_specs=[pl.BlockSpec((rows, d_model), lambda l, j: (0, 0)),
                  pl.BlockSpec((None, d_model, tn), lambda l, j: (l, 0, j)),
                  pl.BlockSpec((None, 1, tn), lambda l, j: (l, 0, j))],
        out_specs=pl.BlockSpec((None, rows, tn), lambda l, j: (l, 0, j)),
        out_shape=jax.ShapeDtypeStruct((depth, rows, n), f32),
        compiler_params=_cparams("arbitrary", "arbitrary"),
        name="ada_modulation",
    )(c_all, w_ada, b_ada.reshape(depth, 1, n))


def _norm_mod(x, gain, scale, shift):
    ms = jnp.mean(x * x, axis=-1, keepdims=True)
    return (x * lax.rsqrt(ms + RMS_EPS) * gain) * (1.0 + scale) + shift


def _nmm_kernel(x_ref, g_ref, sc_ref, sh_ref, w_ref, o_ref, h_ref):
    @pl.when(pl.program_id(1) == 0)
    def _():
        h_ref[...] = _norm_mod(x_ref[...], g_ref[...], sc_ref[...], sh_ref[...]).astype(bf16)

    o_ref[...] = jnp.dot(h_ref[...], w_ref[...], preferred_element_type=f32)


def norm_mod_matmul(x, gain, mod, sc_blk, sh_blk, w, tm, tn):
    m, d = x.shape
    n = w.shape[1]
    per_row = mod.shape[0] != 1
    mrows = tm if per_row else 1
    mod_spec = lambda blk: pl.BlockSpec((mrows, d), (lambda i, j: (i, blk)) if per_row else (lambda i, j: (0, blk)))
    return pl.pallas_call(
        _nmm_kernel,
        grid=(m // tm, n // tn),
        in_specs=[pl.BlockSpec((tm, d), lambda i, j: (i, 0)),
                  pl.BlockSpec((1, d), lambda i, j: (0, 0)),
                  mod_spec(sc_blk), mod_spec(sh_blk),
                  pl.BlockSpec((d, tn), lambda i, j: (0, j))],
        out_specs=pl.BlockSpec((tm, tn), lambda i, j: (i, j)),
        out_shape=jax.ShapeDtypeStruct((m, n), f32),
        scratch_shapes=[pltpu.VMEM((tm, d), bf16)],
        compiler_params=_cparams("arbitrary", "arbitrary"),
        name="norm_mod_matmul",
    )(x, gain.reshape(1, d), mod, mod, w)


def _tri_inverse(lmat, blk):
    n = lmat.shape[0]
    row = lax.broadcasted_iota(jnp.int32, (n, n), 0)
    col = lax.broadcasted_iota(jnp.int32, (n, n), 1)
    eye = jnp.where(row == col, 1.0, 0.0).astype(f32)
    diag = jnp.where(row // SUBLANES == col // SUBLANES, lmat, 0.0)
    d2 = _dot3(diag, diag)
    d4 = _dot3(d2, d2)
    t = _dot3(_dot3(eye - diag, eye + d2), eye + d4)
    b = SUBLANES
    while b < blk:
        off = jnp.where((row // (2 * b) == col // (2 * b)) & (row // b != col // b), lmat, 0.0)
        t = t - _dot3(_dot3(t, off), t)
        b *= 2
    return t


def _cumsum_rows(x):
    row = lax.broadcasted_iota(jnp.int32, x.shape, 0)
    s = 1
    while s < x.shape[0]:
        x = x + jnp.where(row >= s, pltpu.roll(x, s, 0), 0.0)
        s *= 2
    return x


def _gdn_kernel(qkv_ref, z_ref, ab_ref, hist_ref, s0_ref, cw_ref, alog_ref, dtb_ref, nw_ref,
                o_ref, sfin_ref, s_scr, carry, *, chunk, n_valid):
    c_idx = pl.program_id(1)
    n_seq, rows, _ = qkv_ref.shape
    c = chunk
    n_sub = rows // c
    per_group = MXU_DIM // (H_GDN * c)
    units = [(s, u) for s in range(n_seq) for u in range(n_sub)]
    assert len(units) % per_group == 0

    @pl.when(c_idx == 0)
    def _():
        s_scr[...] = s0_ref[...]
        carry[...] = hist_ref[...]

    rowid = lax.broadcasted_iota(jnp.int32, (c, LANES), 0)
    valid = rowid < n_valid
    nw = nw_ref[...]

    prep = {}
    for s in range(n_seq):
        raw = qkv_ref[s]
        ext = jnp.concatenate([carry[s], raw], axis=0)
        carry[s] = raw[rows - SUBLANES:, :]
        conv = ext[SUBLANES:] * cw_ref[GDN_CONV - 1:GDN_CONV, :]
        for i in range(1, GDN_CONV):
            conv = conv + pltpu.roll(ext, i, 0)[SUBLANES:] * cw_ref[GDN_CONV - 1 - i:GDN_CONV - i, :]
        qkv_all = _silu(conv)
        for u in range(n_sub):
            qkv = qkv_all[u * c:(u + 1) * c]
            ab = ab_ref[s, u * c:(u + 1) * c, :]
            x = ab + dtb_ref[...]
            sp = jnp.maximum(x, 0.0) + jnp.log1p(jnp.exp(-jnp.abs(x)))
            g_all = jnp.where(valid, -jnp.exp(alog_ref[...]) * sp, 0.0)
            beta_all = jnp.where(valid, 1.0 / (1.0 + jnp.exp(-ab)), 0.0)
            gc_all = _cumsum_rows(g_all)
            heads = []
            for h in range(H_GDN):
                q = qkv[:, h * HEAD_DIM:(h + 1) * HEAD_DIM]
                k = qkv[:, W_GDN + h * HEAD_DIM:W_GDN + (h + 1) * HEAD_DIM]
                v = qkv[:, 2 * W_GDN + h * HEAD_DIM:2 * W_GDN + (h + 1) * HEAD_DIM]
                q = q * lax.rsqrt(jnp.sum(q * q, axis=-1, keepdims=True) + 1e-6) * HEAD_DIM ** -0.5
                k = k * lax.rsqrt(jnp.sum(k * k, axis=-1, keepdims=True) + 1e-6)
                beta = beta_all[:, H_GDN + h:H_GDN + h + 1]
                gc = gc_all[:, h:h + 1]
                g_last = gc_all[c - 1:c, h:h + 1]
                heads.append(dict(q=q, k=k, kb=k * beta, vb=v * beta, gc=gc, g_last=g_last))
            prep[(s, u)] = heads

    n = MXU_DIM
    row = lax.broadcasted_iota(jnp.int32, (n, n), 0)
    col = lax.broadcasted_iota(jnp.int32, (n, n), 1)
    same = row // c == col // c
    lower = same & (row >= col)
    strict = same & (row > col)

    groups = []
    for g in range(len(units) // per_group):
        pieces = [(su, h) for su in units[g * per_group:(g + 1) * per_group] for h in range(H_GDN)]
        cat = lambda key: jnp.concatenate([prep[su][h][key] for su, h in pieces], axis=0)
        k_all, kb_all, gcol = cat("k"), cat("kb"), cat("gc")
        grow = jnp.broadcast_to(gcol, (n, LANES)).T[0:1, :]
        decay = jnp.where(lower, jnp.exp(jnp.where(lower, gcol - grow, 0.0)), 0.0)
        lmat = jnp.where(strict, _dot_nt(kb_all, k_all) * decay, 0.0)
        tinv = _tri_inverse(lmat, c)
        rhs = jnp.concatenate([cat("vb"), kb_all * jnp.exp(gcol)], axis=1)
        uw = _dot(tinv, rhs)
        qk = _dot_nt(cat("q"), k_all) * decay
        groups.append((pieces, uw, qk))

    for pieces, uw, qk in groups:
        v_new = []
        for j, ((s, u), h) in enumerate(pieces):
            blk = uw[j * c:(j + 1) * c]
            v_new.append(blk[:, :HEAD_DIM] - _dot(blk[:, HEAD_DIM:], s_scr[s, h]))
        o_intra = _dot(qk, jnp.concatenate(v_new, axis=0))
        for j, ((s, u), h) in enumerate(pieces):
            p = prep[(s, u)][h]
            st = s_scr[s, h]
            o = _dot(p["q"] * jnp.exp(p["gc"]), st) + o_intra[j * c:(j + 1) * c]
            s_scr[s, h] = st * jnp.exp(p["g_last"]) + _dot_tn(p["k"] * jnp.exp(p["g_last"] - p["gc"]), v_new[j])
            o = o * lax.rsqrt(jnp.mean(o * o, axis=-1, keepdims=True) + RMS_EPS) * nw
            sl = slice(h * HEAD_DIM, (h + 1) * HEAD_DIM)
            o_ref[s, u * c:(u + 1) * c, sl] = o * _silu(z_ref[s, u * c:(u + 1) * c, sl])

    @pl.when(c_idx == pl.num_programs(1) - 1)
    def _():
        sfin_ref[...] = s_scr[...]


def gdn_mixer(proj, hist, s0, conv_w, a_log, dt_bias, norm_w, chunk, n_valid, n_seq, n_sub):
    s_n, t, _ = proj.shape
    rows = n_sub * chunk
    assert s_n % n_seq == 0 and t % rows == 0 and (n_seq * n_sub * H_GDN * chunk) % MXU_DIM == 0
    lane_pad = lambda a: jnp.zeros((1, LANES), f32).at[0, :H_GDN].set(a)
    return pl.pallas_call(
        functools.partial(_gdn_kernel, chunk=chunk, n_valid=n_valid),
        grid=(s_n // n_seq, t // rows),
        in_specs=[pl.BlockSpec((n_seq, rows, 3 * W_GDN), lambda s, c: (s, c, COL_QKV_A // (3 * W_GDN))),
                  pl.BlockSpec((n_seq, rows, W_GDN), lambda s, c: (s, c, COL_Z_A // W_GDN)),
                  pl.BlockSpec((n_seq, rows, LANES), lambda s, c: (s, c, COL_AB // LANES)),
                  pl.BlockSpec((n_seq, SUBLANES, 3 * W_GDN), lambda s, c: (s, 0, 0)),
                  pl.BlockSpec((n_seq, H_GDN, HEAD_DIM, HEAD_DIM), lambda s, c: (s, 0, 0, 0)),
                  pl.BlockSpec((GDN_CONV, 3 * W_GDN), lambda s, c: (0, 0)),
                  pl.BlockSpec((1, LANES), lambda s, c: (0, 0)),
                  pl.BlockSpec((1, LANES), lambda s, c: (0, 0)),
                  pl.BlockSpec((1, HEAD_DIM), lambda s, c: (0, 0))],
        out_specs=[pl.BlockSpec((n_seq, rows, W_GDN), lambda s, c: (s, c, 0)),
                   pl.BlockSpec((n_seq, H_GDN, HEAD_DIM, HEAD_DIM), lambda s, c: (s, 0, 0, 0))],
        out_shape=[jax.ShapeDtypeStruct((s_n, t, W_GDN), f32),
                   jax.ShapeDtypeStruct((s_n, H_GDN, HEAD_DIM, HEAD_DIM), f32)],
        scratch_shapes=[pltpu.VMEM((n_seq, H_GDN, HEAD_DIM, HEAD_DIM), f32),
                        pltpu.VMEM((n_seq, SUBLANES, 3 * W_GDN), f32)],
        compiler_params=_cparams("arbitrary", "arbitrary"),
        name="gdn_mixer",
    )(proj, proj, proj, hist, s0, conv_w, lane_pad(a_log), lane_pad(dt_bias), norm_w.reshape(1, HEAD_DIM))


def _ret_kernel(q_ref, k_ref, v_ref, g_ref, cs_ref, sn_ref, s0_ref, o_ref, sfin_ref, s_scr, *, n_valid):
    c_idx = pl.program_id(1)
    n_seq, c, _ = q_ref.shape
    c_len = min(c, n_valid)

    @pl.when(c_idx == 0)
    def _():
        s_scr[...] = s0_ref[...]

    cs = cs_ref[...]
    sn = sn_ref[...]
    rot = lambda x: x * cs + pltpu.roll(x, HEAD_DIM // 2, 1) * sn
    r2 = lax.broadcasted_iota(jnp.int32, (c, c), 0)
    c2 = lax.broadcasted_iota(jnp.int32, (c, c), 1)
    rel = (r2 - c2).astype(f32)
    idx = lax.broadcasted_iota(jnp.int32, (c, 1), 0)
    idx_f = idx.astype(f32)
    valid = idx < n_valid

    for h in range(H_RET):
        lg = math.log(1.0 - 2.0 ** (-5.0 - h))
        sl = slice(h * HEAD_DIM, (h + 1) * HEAD_DIM)
        dmat = jnp.where(rel >= 0, jnp.exp(jnp.maximum(rel, 0.0) * lg), 0.0)
        xi = jnp.exp((idx_f + 1.0) * lg)
        zeta = jnp.where(valid, jnp.exp((c_len - 1.0 - idx_f) * lg), 0.0)
        for s in range(n_seq):
            q = rot(q_ref[s, :, sl])
            k = rot(k_ref[s, :, sl]) * HEAD_DIM ** -0.5
            v = v_ref[s, :, sl]
            st = s_scr[s, h]
            o = _dot(_dot_nt(q, k) * dmat, v) + _dot(q * xi, st)
            s_scr[s, h] = math.exp(c_len * lg) * st + _dot_tn(k * zeta, v)
            o = o * lax.rsqrt(jnp.mean(o * o, axis=-1, keepdims=True) + RMS_EPS)
            o_ref[s, :, sl] = o * _silu(g_ref[s, :, sl])

    @pl.when(c_idx == pl.num_programs(1) - 1)
    def _():
        sfin_ref[...] = s_scr[...]


def retention_mixer(proj, cs, sn, s0, chunk, n_valid, n_seq):
    s_n, t, _ = proj.shape
    assert s_n % n_seq == 0 and t % chunk == 0
    col_spec = lambda part: pl.BlockSpec((n_seq, chunk, W_RET), lambda s, c: (s, c, COL_QKVG_C // W_RET + part))
    return pl.pallas_call(
        functools.partial(_ret_kernel, n_valid=n_valid),
        grid=(s_n // n_seq, t // chunk),
        in_specs=[col_spec(0), col_spec(1), col_spec(2), col_spec(3),
                  pl.BlockSpec((chunk, HEAD_DIM), lambda s, c: (c, 0)),
                  pl.BlockSpec((chunk, HEAD_DIM), lambda s, c: (c, 0)),
                  pl.BlockSpec((n_seq, H_RET, HEAD_DIM, HEAD_DIM), lambda s, c: (s, 0, 0, 0))],
        out_specs=[pl.BlockSpec((n_seq, chunk, W_RET), lambda s, c: (s, c, 0)),
                   pl.BlockSpec((n_seq, H_RET, HEAD_DIM, HEAD_DIM), lambda s, c: (s, 0, 0, 0))],
        out_shape=[jax.ShapeDtypeStruct((s_n, t, W_RET), f32),
                   jax.ShapeDtypeStruct((s_n, H_RET, HEAD_DIM, HEAD_DIM), f32)],
        scratch_shapes=[pltpu.VMEM((n_seq, H_RET, HEAD_DIM, HEAD_DIM), f32)],
        compiler_params=_cparams("arbitrary", "arbitrary"),
        name="retention_mixer",
    )(proj, proj, proj, proj, cs, sn, s0)


def _rotary_tables(pos):
    half = HEAD_DIM // 2
    inv = 1.0 / (10000.0 ** jnp.linspace(0.0, 1.0, half, dtype=f32))
    ang = pos.astype(f32)[:, None] * inv[None, :]
    cos, sin = jnp.cos(ang), jnp.sin(ang)
    return jnp.concatenate([cos, cos], axis=1), jnp.concatenate([-sin, sin], axis=1)


def _t5_bias_shifted(dist, tab):
    val = tab(0)
    for b in range(1, NUM_BUCKETS):
        val = jnp.where(dist >= _BUCKET_START[b], tab(b), val)
    return val - tab(NUM_BUCKETS - 1)


def _top_blocks(gate, n_sel):
    lane = lax.broadcasted_iota(jnp.int32, gate.shape, 1)
    sel = jnp.zeros(gate.shape, f32)
    for _ in range(n_sel):
        m = jnp.max(gate, axis=1, keepdims=True)
        first = jnp.min(jnp.where(gate == m, lane, gate.shape[1]), axis=1, keepdims=True)
        pick = (lane == first) & (m > 0.5 * NEG_INF)
        sel = jnp.where(pick, 1.0, sel)
        gate = jnp.where(pick, NEG_INF, gate)
    return sel


def _scores(q_aug, k_aug):
    return lax.dot_general(q_aug, k_aug, (((1,), (1,)), ((), ())), preferred_element_type=f32)


def _moba_prompt_kernel(tab_ref, q_ref, k_ref, v_ref, o_ref,
                        kaug_scr, v_scr, kmean_scr, bias_scr, s_scr, l_scr, acc_scr):
    h = pl.program_id(0)
    i = pl.program_id(1)
    blk = MOBA_BLOCK
    t = k_ref.shape[0]
    n_blocks = t // blk
    ck = FAR_CHUNK * blk
    masked_lane = LANES - 1

    @pl.when(i == 0)
    def _():
        k = k_ref[...]
        kmean_scr[...] = jnp.zeros(kmean_scr.shape, f32)
        kmean_scr[0:n_blocks, :] = jnp.mean(k.reshape(n_blocks, blk, HEAD_DIM), axis=1)
        row = lax.broadcasted_iota(jnp.int32, (t, LANES), 0)
        lane_t = lax.broadcasted_iota(jnp.int32, (t, LANES), 1)
        kaug_scr[blk:blk + t, 0:HEAD_DIM] = k.astype(bf16)
        kaug_scr[blk:blk + t, HEAD_DIM:] = jnp.where(lane_t == row // blk, 1.0, 0.0).astype(bf16)
        v_scr[blk:blk + t, :] = v_ref[...].astype(bf16)
        for start, size in ((0, blk), (blk + t, kaug_scr.shape[0] - blk - t)):
            if not size:
                continue
            lane_p = lax.broadcasted_iota(jnp.int32, (size, LANES), 1)
            kaug_scr[start:start + size, 0:HEAD_DIM] = jnp.zeros((size, HEAD_DIM), bf16)
            kaug_scr[start:start + size, HEAD_DIM:] = jnp.where(lane_p == masked_lane, 1.0, 0.0).astype(bf16)
            v_scr[start:start + size, :] = jnp.zeros((size, HEAD_DIM), bf16)
        r2 = lax.broadcasted_iota(jnp.int32, (blk, blk), 0)
        c2 = lax.broadcasted_iota(jnp.int32, (blk, blk), 1)
        tab = lambda b: tab_ref[b, h]
        bias_scr[0] = _t5_bias_shifted(r2 - c2 + blk, tab)
        bias_scr[1] = jnp.where(r2 >= c2, _t5_bias_shifted(r2 - c2, tab), NEG_INF)

    q = q_ref[...]
    lane = lax.broadcasted_iota(jnp.int32, (blk, LANES), 1)
    gate = jnp.where(lane < i, _dot3_nt(q, kmean_scr[...]), NEG_INF)
    sel = _top_blocks(gate, MOBA_TOPK) > 0.5
    qs = q * HEAD_DIM ** -0.5
    neg_far = jnp.where(sel & (lane < i - 1), 0.0, NEG_INF)
    neg_near = jnp.where((sel & (lane == i - 1)) | (lane == i), 0.0, NEG_INF)
    q_far = jnp.concatenate([qs, neg_far], axis=1).astype(bf16)
    q_near = jnp.concatenate([qs, neg_near], axis=1).astype(bf16)
    n_chunks = (jnp.maximum(i - 1, 0) + FAR_CHUNK - 1) // FAR_CHUNK

    far_rows = lambda c: pl.ds(pl.multiple_of(blk + c * ck, blk), ck)

    def scores_pass(c, mx):
        s = _scores(q_far, kaug_scr[far_rows(c), :])
        s_scr[c] = s
        return jnp.maximum(mx, _fold(s, jnp.maximum))

    mx = lax.fori_loop(0, n_chunks, scores_pass, jnp.full((blk, LANES), NEG_INF, f32))
    near_rows = pl.ds(pl.multiple_of(i * blk, blk), 2 * blk)
    s_near = _scores(q_near, kaug_scr[near_rows, :]) + jnp.concatenate([bias_scr[0], bias_scr[1]], axis=1)
    m = jnp.max(jnp.maximum(mx, _fold(s_near, jnp.maximum)), axis=1, keepdims=True)
    p_near = jnp.exp(s_near - m)
    l_scr[...] = _fold(p_near, jnp.add)
    acc_scr[...] = jnp.dot(p_near.astype(bf16), v_scr[near_rows, :], preferred_element_type=f32)

    def values_pass(c, carry):
        p = jnp.exp(s_scr[c] - m)
        l_scr[...] += _fold(p, jnp.add)
        acc_scr[...] += jnp.dot(p.astype(bf16), v_scr[far_rows(c), :], preferred_element_type=f32)
        return carry

    lax.fori_loop(0, n_chunks, values_pass, 0)
    o_ref[...] = acc_scr[...] / jnp.sum(l_scr[...], axis=1, keepdims=True)


def moba_prompt(proj, rel_bias):
    t = proj.shape[0]
    blk = MOBA_BLOCK
    n_blocks = t // blk
    assert t % blk == 0 and n_blocks < LANES - 1 and blk >= _BUCKET_START[-1]
    qb, kb, vb = (COL_QKV_B // HEAD_DIM, COL_QKV_B // HEAD_DIM + H_MOBA, COL_QKV_B // HEAD_DIM + 2 * H_MOBA)
    far_max = -(-max(n_blocks - 2, 1) // FAR_CHUNK) * FAR_CHUNK
    kv_rows = (1 + max(n_blocks, far_max + 1)) * blk
    return pl.pallas_call(
        _moba_prompt_kernel,
        grid=(H_MOBA, n_blocks),
        in_specs=[pl.BlockSpec(memory_space=pltpu.SMEM),
                  pl.BlockSpec((blk, HEAD_DIM), lambda h, i: (i, qb + h)),
                  pl.BlockSpec((t, HEAD_DIM), lambda h, i: (0, kb + h)),
                  pl.BlockSpec((t, HEAD_DIM), lambda h, i: (0, vb + h))],
        out_specs=pl.BlockSpec((blk, HEAD_DIM), lambda h, i: (i, h)),
        out_shape=jax.ShapeDtypeStruct((t, W_MOBA), f32),
        scratch_shapes=[pltpu.VMEM((kv_rows, 2 * HEAD_DIM), bf16),
                        pltpu.VMEM((kv_rows, HEAD_DIM), bf16),
                        pltpu.VMEM((LANES, HEAD_DIM), f32),
                        pltpu.VMEM((2, blk, blk), f32),
                        pltpu.VMEM((far_max // FAR_CHUNK, blk, FAR_CHUNK * blk), f32),
                        pltpu.VMEM((blk, LANES), f32),
                        pltpu.VMEM((blk, HEAD_DIM), f32)],
        compiler_params=_cparams("arbitrary", "arbitrary"),
        name="moba_prompt",
    )(rel_bias, proj, proj, proj)


def _moba_sample_kernel(pt_ref, q_ref, kn_ref, vn_ref, tabr_ref, *rest, n_pages, n_new):
    kp = rest[:n_pages]
    vp = rest[n_pages:2 * n_pages]
    o_ref, onehot_scr, near_scr, own_scr, s_scr = rest[2 * n_pages:]
    page = kp[0].shape[0]
    pr = page * H_MOBA
    past = n_pages * page
    ppb = MOBA_BLOCK // page
    n_blocks = n_pages // ppb
    rows = H_MOBA * n_new
    n_own = kn_ref.shape[0]
    near = [p for p in range(n_pages) if past - (p * page + page - 1) < _BUCKET_START[-1]]

    r_l = lax.broadcasted_iota(jnp.int32, (rows, LANES), 0)
    l_l = lax.broadcasted_iota(jnp.int32, (rows, LANES), 1)
    head_r = r_l // n_new
    t_r = r_l % n_new

    @pl.when(pl.program_id(0) == 0)
    def _():
        kr = lax.broadcasted_iota(jnp.int32, (pr, LANES), 0)
        kl = lax.broadcasted_iota(jnp.int32, (pr, LANES), 1)
        for n in range(n_blocks):
            onehot_scr[n] = jnp.where(kl == n * H_MOBA + kr % H_MOBA, 1.0, 0.0).astype(bf16)
        tabr = tabr_ref[...]
        tab = lambda b: tabr[:, b:b + 1]
        rr = lax.broadcasted_iota(jnp.int32, (rows, pr), 0)
        cc = lax.broadcasted_iota(jnp.int32, (rows, pr), 1)
        for idx, p in enumerate(near):
            near_scr[idx] = _t5_bias_shifted(past + rr % n_new - (p * page + cc // H_MOBA), tab)
        t_k = l_l // H_MOBA
        visible = (l_l % H_MOBA == head_r) & (t_k <= t_r) & (t_k < n_new)
        own_scr[...] = jnp.where(visible, _t5_bias_shifted(t_r - t_k, tab), NEG_INF)

    q = q_ref[...]
    kmean = []
    for n in range(n_blocks):
        tot = functools.reduce(jnp.add, [jnp.sum(kp[p][...], axis=0) for p in range(n * ppb, (n + 1) * ppb)])
        kmean.append(tot * (1.0 / MOBA_BLOCK))
    kmean.append(jnp.zeros((LANES - n_blocks * H_MOBA, HEAD_DIM), f32))
    gate = _dot3_nt(q, jnp.concatenate(kmean, axis=0))
    own_head = (l_l < n_blocks * H_MOBA) & (l_l % H_MOBA == head_r)
    sel = _top_blocks(jnp.where(own_head, gate, NEG_INF), min(MOBA_TOPK, n_blocks))
    qs = q * HEAD_DIM ** -0.5
    q_aug = jnp.concatenate([qs, jnp.where(sel > 0.5, 0.0, NEG_INF)], axis=1).astype(bf16)

    mx = jnp.full((rows, LANES), NEG_INF, f32)
    for p in range(n_pages):
        k_aug = jnp.concatenate([kp[p][...].reshape(pr, HEAD_DIM).astype(bf16), onehot_scr[p // ppb]], axis=1)
        s = _scores(q_aug, k_aug)
        if p in near:
            s = s + near_scr[near.index(p)]
        s_scr[:, p * pr:(p + 1) * pr] = s
        mx = jnp.maximum(mx, _fold(s, jnp.maximum))
    pad_own = lambda r: jnp.concatenate([r[...], jnp.zeros((LANES - n_own, HEAD_DIM), f32)], axis=0)
    s_own = _dot_nt(qs, pad_own(kn_ref)) + own_scr[...]
    m = jnp.max(jnp.maximum(mx, s_own), axis=1, keepdims=True)
    p_own = jnp.exp(s_own - m)
    lsum = p_own
    acc = _dot(p_own, pad_own(vn_ref))
    for p in range(n_pages):
        pe = jnp.exp(s_scr[:, p * pr:(p + 1) * pr] - m)
        lsum = lsum + _fold(pe, jnp.add)
        acc = acc + _dot(pe, vp[p][...].reshape(pr, HEAD_DIM))
    o_ref[...] = acc / jnp.sum(lsum, axis=1, keepdims=True)


def moba_sample(q, k_new, v_new, cache_k, cache_v, page_table, rel_bias, layer, n_new):
    bsz, rows, _ = q.shape
    n_pages = page_table.shape[1]
    page = cache_k.shape[2]
    past = n_pages * page
    n_blocks = past // MOBA_BLOCK
    assert past % MOBA_BLOCK == 0 and MOBA_BLOCK % page == 0 and n_blocks * H_MOBA <= LANES
    assert rows == H_MOBA * n_new and k_new.shape[1] <= LANES
    n_near = len([p for p in range(n_pages) if past - (p * page + page - 1) < _BUCKET_START[-1]])
    tab_rows = jnp.pad(jnp.repeat(rel_bias.T, n_new, axis=0), ((0, 0), (0, LANES - NUM_BUCKETS)))
    page_spec = lambda p: pl.BlockSpec((None, None, page, H_MOBA, HEAD_DIM),
                                       lambda b, pt: (layer, pt[b, p], 0, 0, 0))
    row_spec = lambda a: pl.BlockSpec((None,) + a.shape[1:], lambda b, pt: (b, 0, 0))
    grid_spec = pltpu.PrefetchScalarGridSpec(
        num_scalar_prefetch=1,
        grid=(bsz,),
        in_specs=([row_spec(q), row_spec(k_new), row_spec(v_new),
                   pl.BlockSpec(tab_rows.shape, lambda b, pt: (0, 0))]
                  + [page_spec(p) for p in range(n_pages)] * 2),
        out_specs=row_spec(q),
        scratch_shapes=[pltpu.VMEM((n_blocks, page * H_MOBA, LANES), bf16),
                        pltpu.VMEM((max(n_near, 1), rows, page * H_MOBA), f32),
                        pltpu.VMEM((rows, LANES), f32),
                        pltpu.VMEM((rows, past * H_MOBA), f32)],
    )
    return pl.pallas_call(
        functools.partial(_moba_sample_kernel, n_pages=n_pages, n_new=n_new),
        grid_spec=grid_spec,
        out_shape=jax.ShapeDtypeStruct(q.shape, f32),
        compiler_params=_cparams("arbitrary"),
        name="moba_sample",
    )(page_table, q, k_new, v_new, tab_rows, *([cache_k] * n_pages), *([cache_v] * n_pages))


def _post(x, y, gain, gate):
    return x + gate * (y * lax.rsqrt(jnp.mean(y * y, axis=-1, keepdims=True) + RMS_EPS) * gain)


def _out_proj_kernel(oa_ref, ob_ref, oc_ref, w_ref, x_ref, gt_ref, g_ref, o_ref):
    y = (jnp.dot(oa_ref[...].astype(bf16), w_ref[0:W_GDN, :], preferred_element_type=f32)
         + jnp.dot(ob_ref[...].astype(bf16), w_ref[W_GDN:W_GDN + W_MOBA, :], preferred_element_type=f32)
         + jnp.dot(oc_ref[...].astype(bf16), w_ref[W_GDN + W_MOBA:, :], preferred_element_type=f32))
    o_ref[...] = _post(x_ref[...], y, g_ref[...], gt_ref[...])


def out_proj_residual(o_a, o_b, o_c, w_out, x, mod, gt_blk, gain, tm):
    m, d = x.shape
    per_row = mod.shape[0] != 1
    mod_spec = pl.BlockSpec((tm if per_row else 1, d), (lambda i: (i, gt_blk)) if per_row else (lambda i: (0, gt_blk)))
    return pl.pallas_call(
        _out_proj_kernel,
        grid=(m // tm,),
        in_specs=[pl.BlockSpec((tm, W_GDN), lambda i: (i, 0)),
                  pl.BlockSpec((tm, W_MOBA), lambda i: (i, 0)),
                  pl.BlockSpec((tm, W_RET), lambda i: (i, 0)),
                  pl.BlockSpec(w_out.shape, lambda i: (0, 0)),
                  pl.BlockSpec((tm, d), lambda i: (i, 0)),
                  mod_spec,
                  pl.BlockSpec((1, d), lambda i: (0, 0))],
        out_specs=pl.BlockSpec((tm, d), lambda i: (i, 0)),
        out_shape=jax.ShapeDtypeStruct((m, d), f32),
        compiler_params=_cparams("arbitrary"),
        name="out_proj_residual",
    )(o_a, o_b, o_c, w_out, x, mod, gain.reshape(1, d))


def _ffn_down_kernel(ug_ref, uu_ref, pg_ref, pu_ref, hg_ref, hu_ref, cwg_ref, cwu_ref, wd_ref,
                     x_ref, gt_ref, g_ref, o_ref, acc_ref, *, row_shift):
    i = pl.program_id(0)
    f = pl.program_id(1)
    hdr = hg_ref.shape[0]
    first = i == 0

    def conv(u_ref, p_ref, h_ref, cw_ref):
        head = jnp.where(first, h_ref[...], p_ref[...])
        ext = jnp.concatenate([head, u_ref[...]], axis=0)
        out = ext[hdr:] * cw_ref[FFN_CONV - 1:FFN_CONV, :]
        for j in range(1, FFN_CONV):
            out = out + pltpu.roll(ext, j * row_shift, 0)[hdr:] * cw_ref[FFN_CONV - 1 - j:FFN_CONV - j, :]
        return out

    act = _silu(conv(ug_ref, pg_ref, hg_ref, cwg_ref)) * conv(uu_ref, pu_ref, hu_ref, cwu_ref)
    part = jnp.dot(act.astype(bf16), wd_ref[...], preferred_element_type=f32)

    @pl.when(f == 0)
    def _():
        acc_ref[...] = part

    @pl.when(f > 0)
    def _():
        acc_ref[...] += part

    @pl.when(f == pl.num_programs(1) - 1)
    def _():
        o_ref[...] = _post(x_ref[...], acc_ref[...], g_ref[...], gt_ref[...])


def ffn_down_residual(u, hist, conv_w, w_down, x, mod, gt_blk, gain, tm, tf, row_shift):
    m, d = x.shape
    ff = w_down.shape[0]
    nf = ff // tf
    hdr = hist.shape[0]
    assert hdr >= (FFN_CONV - 1) * row_shift and tm % hdr == 0
    per_row = mod.shape[0] != 1
    mod_spec = pl.BlockSpec((tm if per_row else 1, d),
                            (lambda i, f: (i, gt_blk)) if per_row else (lambda i, f: (0, gt_blk)))
    prev = lambda off: (lambda i, f: (jnp.maximum(i * (tm // hdr) - 1, 0), f + off))
    return pl.pallas_call(
        functools.partial(_ffn_down_kernel, row_shift=row_shift),
        grid=(m // tm, nf),
        in_specs=[pl.BlockSpec((tm, tf), lambda i, f: (i, f)),
                  pl.BlockSpec((tm, tf), lambda i, f: (i, f + nf)),
                  pl.BlockSpec((hdr, tf), prev(0)),
                  pl.BlockSpec((hdr, tf), prev(nf)),
                  pl.BlockSpec((hdr, tf), lambda i, f: (0, f)),
                  pl.BlockSpec((hdr, tf), lambda i, f: (0, f + nf)),
                  pl.BlockSpec((FFN_CONV, tf), lambda i, f: (0, f)),
                  pl.BlockSpec((FFN_CONV, tf), lambda i, f: (0, f + nf)),
                  pl.BlockSpec((tf, d), lambda i, f: (f, 0)),
                  pl.BlockSpec((tm, d), lambda i, f: (i, 0)),
                  mod_spec,
                  pl.BlockSpec((1, d), lambda i, f: (0, 0))],
        out_specs=pl.BlockSpec((tm, d), lambda i, f: (i, 0)),
        out_shape=jax.ShapeDtypeStruct((m, d), f32),
        scratch_shapes=[pltpu.VMEM((tm, d), f32)],
        compiler_params=_cparams("arbitrary", "arbitrary"),
        name="ffn_down_residual",
    )(u, u, u, u, hist, hist, conv_w, conv_w, w_down, x, mod, gain.reshape(1, d))


def _permute_w_in(w_in):
    s1 = 3 * W_GDN
    s2 = s1 + W_GDN
    s4 = s2 + 2 * H_GDN
    ab = w_in[..., s2:s4]
    pad = jnp.zeros(w_in.shape[:-1] + (PROJ_W - COL_AB - 2 * H_GDN,), w_in.dtype)
    return jnp.concatenate([w_in[..., :s2], w_in[..., s4:], ab, pad], axis=-1).astype(bf16)


def kernel(x_prompt, x_sample, c_prompt, c_sample, cache_k, cache_v, page_table, state_gdn_conv, state_gdn, state_ret, state_ffn_conv, w_ada, b_ada, g_pre_mix, g_post_mix, g_pre_ffn, g_post_ffn, w_in, gdn_conv_w, gdn_a_log, gdn_dt_bias, gdn_norm_w, rel_bias, w_out, w_up, ffn_conv_w, w_down):
    bp, seq, d = x_prompt.shape
    bs, dec = x_sample.shape[:2]
    depth = w_in.shape[0]
    assert bp == 1 and dec <= SUBLANES
    past_len = page_table.shape[1] * cache_k.shape[2]
    d_ff = w_down.shape[1]
    pad_t = SUBLANES - dec
    seq_per_step = MXU_DIM // (H_GDN * SUBLANES)

    w_in_b = _permute_w_in(w_in)
    w_out_b = w_out.astype(bf16)
    w_up_b = w_up.astype(bf16)
    w_down_b = w_down.astype(bf16)

    n_c = bp + bs
    c_all = jnp.concatenate([c_prompt, c_sample, jnp.zeros((-n_c % SUBLANES, d), f32)], axis=0)
    mod = ada_modulation(c_all, w_ada, b_ada)

    cs_p, sn_p = _rotary_tables(jnp.arange(seq, dtype=jnp.int32))
    cs_s, sn_s = _rotary_tables(past_len + jnp.arange(SUBLANES, dtype=jnp.int32))

    yp = x_prompt.reshape(seq, d)
    ys = jnp.swapaxes(x_sample, 0, 1).reshape(dec * bs, d)
    to_bm = lambda a: jnp.pad(jnp.swapaxes(a.reshape(dec, bs, -1), 0, 1), ((0, 0), (0, pad_t), (0, 0)))
    to_tm = lambda a: jnp.swapaxes(a[:, :dec], 0, 1).reshape(dec * bs, -1)
    heads_tm = lambda a: a.reshape(dec, bs, H_MOBA, HEAD_DIM)

    outs = [[] for _ in range(12)]
    for l in range(depth):
        mod_p = mod[l, 0:bp]
        mod_s = jnp.tile(mod[l, bp:bp + bs], (dec, 1))

        proj_p = norm_mod_matmul(yp, g_pre_mix[l], mod_p, 1, 0, w_in_b[l], tm=1024, tn=PROJ_TILE)
        proj_p3 = proj_p.reshape(1, seq, PROJ_W)
        oa_p, gs_p = gdn_mixer(proj_p3, jnp.zeros((1, SUBLANES, 3 * W_GDN), f32),
                               jnp.zeros((1, H_GDN, HEAD_DIM, HEAD_DIM), f32),
                               gdn_conv_w[l], gdn_a_log[l], gdn_dt_bias[l], gdn_norm_w[l],
                               chunk=GDN_CHUNK, n_valid=GDN_CHUNK, n_seq=1, n_sub=2)
        oc_p, rs_p = retention_mixer(proj_p3, cs_p, sn_p, jnp.zeros((1, H_RET, HEAD_DIM, HEAD_DIM), f32),
                                     chunk=RET_CHUNK, n_valid=RET_CHUNK, n_seq=1)
        ob_p = moba_prompt(proj_p, rel_bias)
        yp = out_proj_residual(oa_p[0], ob_p, oc_p[0], w_out_b[l], yp, mod_p, 2, g_post_mix[l], tm=512)

        proj_s = norm_mod_matmul(ys, g_pre_mix[l], mod_s, 1, 0, w_in_b[l], tm=dec * bs, tn=PROJ_TILE)
        proj_sb = to_bm(proj_s)
        hist_s = jnp.pad(state_gdn_conv[l], ((0, 0), (SUBLANES - (GDN_CONV - 1), 0), (0, 0)))
        oa_s, gs_s = gdn_mixer(proj_sb, hist_s, state_gdn[l], gdn_conv_w[l], gdn_a_log[l], gdn_dt_bias[l],
                               gdn_norm_w[l], chunk=SUBLANES, n_valid=dec, n_seq=seq_per_step, n_sub=1)
        oc_s, rs_s = retention_mixer(proj_sb, cs_s, sn_s, state_ret[l], chunk=SUBLANES, n_valid=dec,
                                     n_seq=seq_per_step)
        q_s, k_s, v_s = [heads_tm(proj_s[:, COL_QKV_B + j * W_MOBA:COL_QKV_B + (j + 1) * W_MOBA]) for j in range(3)]
        k_s, v_s = jnp.swapaxes(k_s, 0, 1), jnp.swapaxes(v_s, 0, 1)
        ob_s = moba_sample(jnp.transpose(q_s, (1, 2, 0, 3)).reshape(bs, H_MOBA * dec, HEAD_DIM),
                           k_s.reshape(bs, dec * H_MOBA, HEAD_DIM), v_s.reshape(bs, dec * H_MOBA, HEAD_DIM),
                           cache_k, cache_v, page_table, rel_bias, l, dec)
        ob_s = jnp.transpose(ob_s.reshape(bs, H_MOBA, dec, HEAD_DIM), (2, 0, 1, 3)).reshape(dec * bs, W_MOBA)
        ys = out_proj_residual(to_tm(oa_s), ob_s, to_tm(oc_s), w_out_b[l], ys, mod_s, 2,
                               g_post_mix[l], tm=dec * bs)

        u_p = norm_mod_matmul(yp, g_pre_ffn[l], mod_p, 4, 3, w_up_b[l], tm=1024, tn=1024)
        yp = ffn_down_residual(u_p, jnp.zeros((SUBLANES, 2 * d_ff), f32), ffn_conv_w[l], w_down_b[l], yp,
                               mod_p, 5, g_post_ffn[l], tm=512, tf=512, row_shift=1)
        u_s = norm_mod_matmul(ys, g_pre_ffn[l], mod_s, 4, 3, w_up_b[l], tm=dec * bs, tn=1024)
        hist_f = jnp.swapaxes(state_ffn_conv[l], 0, 1).reshape((FFN_CONV - 1) * bs, 2 * d_ff)
        ys = ffn_down_residual(u_s, hist_f, ffn_conv_w[l], w_down_b[l], ys, mod_s, 5, g_post_ffn[l],
                               tm=dec * bs, tf=512, row_shift=bs)

        kcol = slice(COL_QKV_B + W_MOBA, COL_QKV_B + 2 * W_MOBA)
        vcol = slice(COL_QKV_B + 2 * W_MOBA, COL_QKV_B + 3 * W_MOBA)
        outs[0].append(proj_p[:, kcol].reshape(bp, seq, H_MOBA, HEAD_DIM))
        outs[1].append(proj_p[:, vcol].reshape(bp, seq, H_MOBA, HEAD_DIM))
        outs[2].append(k_s)
        outs[3].append(v_s)
        outs[4].append(proj_p[seq - (GDN_CONV - 1):, :3 * W_GDN].reshape(bp, GDN_CONV - 1, 3 * W_GDN))
        outs[5].append(proj_sb[:, dec - (GDN_CONV - 1):dec, :3 * W_GDN])
        outs[6].append(gs_p)
        outs[7].append(gs_s)
        outs[8].append(rs_p)
        outs[9].append(rs_s)
        outs[10].append(u_p[seq - (FFN_CONV - 1):].reshape(bp, FFN_CONV - 1, 2 * d_ff))
        outs[11].append(jnp.swapaxes(u_s.reshape(dec, bs, 2 * d_ff)[dec - (FFN_CONV - 1):], 0, 1))

    y_prompt = yp.reshape(bp, seq, d)
    y_sample = jnp.swapaxes(ys.reshape(dec, bs, d), 0, 1)
    return (y_prompt, y_sample) + tuple(jnp.stack(o, axis=0) for o in outs)
```

```python
import functools
import math

import numpy as np
import jax
import jax.numpy as jnp
from jax import lax
from jax.experimental import pallas as pl
from jax.experimental.pallas import tpu as pltpu

f32 = jnp.float32
bf16 = jnp.bfloat16

HEAD_DIM = 128
H_GDN = 4
H_MOBA = 8
H_RET = 4
W_GDN = H_GDN * HEAD_DIM
W_MOBA = H_MOBA * HEAD_DIM
W_RET = H_RET * HEAD_DIM
GDN_CONV = 4
FFN_CONV = 3
MOBA_BLOCK = 256
MOBA_TOPK = 3
NUM_BUCKETS = 32
MAX_DISTANCE = 128
RMS_EPS = 1e-6
NEG_INF = -1e30

SUBLANES = 8
LANES = 128
MXU_DIM = 256
VMEM_LIMIT = 56 * 1024 * 1024

COL_QKV_A = 0
COL_Z_A = 3 * W_GDN
COL_QKV_B = COL_Z_A + W_GDN
COL_QKVG_C = COL_QKV_B + 3 * W_MOBA
COL_AB = COL_QKVG_C + 4 * W_RET
PROJ_TILE = 5 * MXU_DIM
PROJ_W = -(-(COL_AB + LANES) // PROJ_TILE) * PROJ_TILE

SPLIT_TILE = 2 * MXU_DIM
assert all(c % SPLIT_TILE == 0 for c in (COL_Z_A, COL_QKV_B, W_MOBA, COL_QKVG_C, COL_AB, PROJ_W))

GDN_CHUNK = 64
RET_CHUNK = 256
FAR_CHUNK = 4


def _t5_bucket_starts():
    max_exact = NUM_BUCKETS // 2
    d = np.arange(0, 4 * MAX_DISTANCE)
    dd = np.maximum(d, max_exact).astype(np.float64)
    large = max_exact + (np.log(dd / max_exact) / math.log(MAX_DISTANCE / max_exact)
                         * (NUM_BUCKETS - max_exact)).astype(np.int32)
    bucket = np.where(d < max_exact, d, np.minimum(large, NUM_BUCKETS - 1))
    return [int(np.argmax(bucket >= b)) for b in range(NUM_BUCKETS)]


_BUCKET_START = _t5_bucket_starts()


def _cparams(*sem):
    return pltpu.CompilerParams(dimension_semantics=sem, vmem_limit_bytes=VMEM_LIMIT)


def _silu(x):
    return x * (1.0 / (1.0 + jnp.exp(-x)))


def _dot(a, b):
    return jnp.dot(a.astype(bf16), b.astype(bf16), preferred_element_type=f32)


def _dot_nt(a, b):
    return lax.dot_general(a.astype(bf16), b.astype(bf16), (((1,), (1,)), ((), ())),
                           preferred_element_type=f32)


def _dot_tn(a, b):
    return lax.dot_general(a.astype(bf16), b.astype(bf16), (((0,), (0,)), ((), ())),
                           preferred_element_type=f32)


def _split(a):
    hi = a.astype(bf16)
    lo = (a - hi.astype(f32)).astype(bf16)
    return hi, lo


def _dot3(a, b):
    ah, al = _split(a)
    bh, bl = _split(b)
    d = lambda x, y: jnp.dot(x, y, preferred_element_type=f32)
    return d(ah, bh) + (d(ah, bl) + d(al, bh))


def _dot3_nt(a, b):
    ah, al = _split(a)
    bh, bl = _split(b)
    d = lambda x, y: lax.dot_general(x, y, (((1,), (1,)), ((), ())), preferred_element_type=f32)
    return d(ah, bh) + (d(ah, bl) + d(al, bh))


def _fold(x, op):
    return functools.reduce(op, [x[:, i * LANES:(i + 1) * LANES] for i in range(x.shape[1] // LANES)])


def _ada_kernel(c_ref, w_ref, b_ref, o_ref):
    o_ref[...] = _dot(_silu(c_ref[...]), w_ref[...]) + b_ref[...]


def ada_modulation(c_all, w_ada, b_ada, tn=1024):
    depth, d_model, n = w_ada.shape
    rows = c_all.shape[0]
    return pl.pallas_call(
        _ada_kernel,
        grid=(depth, n // tn),
        in_specs=[pl.BlockSpec((rows, d_model), lambda l, j: (0, 0)),
                  pl.BlockSpec((None, d_model, tn), lambda l, j: (l, 0, j)),
                  pl.BlockSpec((None, 1, tn), lambda l, j: (l, 0, j))],
        out_specs=pl.BlockSpec((None, rows, tn), lambda l, j: (l, 0, j)),
        out_shape=jax.ShapeDtypeStruct((depth, rows, n), f32),
        compiler_params=_cparams("arbitrary", "arbitrary"),
        name="ada_modulation",
    )(c_all, w_ada, b_ada.reshape(depth, 1, n))


def _norm_mod(x, gain, scale, shift):
    ms = jnp.mean(x * x, axis=-1, keepdims=True)
    return (x * lax.rsqrt(ms + RMS_EPS) * gain) * (1.0 + scale) + shift


def _nmm_kernel(x_ref, g_ref, sc_ref, sh_ref, w_ref, o_ref, h_ref):
    @pl.when(pl.program_id(1) == 0)
    def _():
        h_ref[...] = _norm_mod(x_ref[...], g_ref[...], sc_ref[...], sh_ref[...]).astype(bf16)

    o_ref[...] = jnp.dot(h_ref[...], w_ref[...], preferred_element_type=f32)


def _nmm_split_kernel(x_ref, g_ref, sc_ref, sh_ref, w_ref, *rest, tile_ranges):
    o_refs, h_ref = rest[:-1], rest[-1]
    j = pl.program_id(1)

    @pl.when(j == 0)
    def _():
        h_ref[...] = _norm_mod(x_ref[...], g_ref[...], sc_ref[...], sh_ref[...]).astype(bf16)

    y = jnp.dot(h_ref[...], w_ref[...], preferred_element_type=f32)
    for o_ref, (lo, hi) in zip(o_refs, tile_ranges):
        @pl.when((j >= lo) & (j < hi))
        def _(o_ref=o_ref):
            o_ref[...] = y


def norm_mod_matmul_split(x, gain, mod, sc_blk, sh_blk, w, tm, tn, widths):
    m, d = x.shape
    assert mod.shape[0] == 1 and sum(widths) == w.shape[1] and all(wd % tn == 0 for wd in widths)
    bounds = np.cumsum([0] + [wd // tn for wd in widths])
    tile_ranges = [(int(lo), int(hi)) for lo, hi in zip(bounds[:-1], bounds[1:])]
    out_spec = lambda lo, hi: pl.BlockSpec((tm, tn), lambda i, j: (i, jnp.clip(j - lo, 0, hi - lo - 1)))
    return pl.pallas_call(
        functools.partial(_nmm_split_kernel, tile_ranges=tile_ranges),
        grid=(m // tm, w.shape[1] // tn),
        in_specs=[pl.BlockSpec((tm, d), lambda i, j: (i, 0)),
                  pl.BlockSpec((1, d), lambda i, j: (0, 0)),
                  pl.BlockSpec((1, d), lambda i, j: (0, sc_blk)),
                  pl.BlockSpec((1, d), lambda i, j: (0, sh_blk)),
                  pl.BlockSpec((d, tn), lambda i, j: (0, j))],
        out_specs=[out_spec(lo, hi) for lo, hi in tile_ranges],
        out_shape=[jax.ShapeDtypeStruct((m, wd), f32) for wd in widths],
        scratch_shapes=[pltpu.VMEM((tm, d), bf16)],
        compiler_params=_cparams("arbitrary", "arbitrary"),
        name="norm_mod_matmul_split",
    )(x, gain.reshape(1, d), mod, mod, w)


def norm_mod_matmul(x, gain, mod, sc_blk, sh_blk, w, tm, tn):
    m, d = x.shape
    n = w.shape[1]
    per_row = mod.shape[0] != 1
    mrows = tm if per_row else 1
    mod_spec = lambda blk: pl.BlockSpec((mrows, d), (lambda i, j: (i, blk)) if per_row else (lambda i, j: (0, blk)))
    return pl.pallas_call(
        _nmm_kernel,
        grid=(m // tm, n // tn),
        in_specs=[pl.BlockSpec((tm, d), lambda i, j: (i, 0)),
                  pl.BlockSpec((1, d), lambda i, j: (0, 0)),
                  mod_spec(sc_blk), mod_spec(sh_blk),
                  pl.BlockSpec((d, tn), lambda i, j: (0, j))],
        out_specs=pl.BlockSpec((tm, tn), lambda i, j: (i, j)),
        out_shape=jax.ShapeDtypeStruct((m, n), f32),
        scratch_shapes=[pltpu.VMEM((tm, d), bf16)],
        compiler_params=_cparams("arbitrary", "arbitrary"),
        name="norm_mod_matmul",
    )(x, gain.reshape(1, d), mod, mod, w)


def _tri_inverse(lmat, blk):
    n = lmat.shape[0]
    row = lax.broadcasted_iota(jnp.int32, (n, n), 0)
    col = lax.broadcasted_iota(jnp.int32, (n, n), 1)
    eye = jnp.where(row == col, 1.0, 0.0).astype(f32)
    diag = jnp.where(row // SUBLANES == col // SUBLANES, lmat, 0.0)
    d2 = _dot3(diag, diag)
    d4 = _dot3(d2, d2)
    t = _dot3(_dot3(eye - diag, eye + d2), eye + d4)
    b = SUBLANES
    while b < blk:
        off = jnp.where((row // (2 * b) == col // (2 * b)) & (row // b != col // b), lmat, 0.0)
        t = t - _dot3(_dot3(t, off), t)
        b *= 2
    return t


def _cumsum_rows(x):
    row = lax.broadcasted_iota(jnp.int32, x.shape, 0)
    s = 1
    while s < x.shape[0]:
        x = x + jnp.where(row >= s, pltpu.roll(x, s, 0), 0.0)
        s *= 2
    return x


def _gdn_kernel(qkv_ref, z_ref, ab_ref, hist_ref, s0_ref, cw_ref, alog_ref, dtb_ref, nw_ref,
                o_ref, sfin_ref, s_scr, carry, *, chunk, n_valid):
    c_idx = pl.program_id(1)
    n_seq, rows, _ = qkv_ref.shape
    c = chunk
    n_sub = rows // c
    per_group = MXU_DIM // (H_GDN * c)
    units = [(s, u) for s in range(n_seq) for u in range(n_sub)]
    assert len(units) % per_group == 0

    @pl.when(c_idx == 0)
    def _():
        s_scr[...] = s0_ref[...]
        carry[...] = hist_ref[...]

    rowid = lax.broadcasted_iota(jnp.int32, (c, LANES), 0)
    valid = rowid < n_valid
    nw = nw_ref[...]

    prep = {}
    for s in range(n_seq):
        raw = qkv_ref[s]
        ext = jnp.concatenate([carry[s], raw], axis=0)
        carry[s] = raw[rows - SUBLANES:, :]
        conv = ext[SUBLANES:] * cw_ref[GDN_CONV - 1:GDN_CONV, :]
        for i in range(1, GDN_CONV):
            conv = conv + pltpu.roll(ext, i, 0)[SUBLANES:] * cw_ref[GDN_CONV - 1 - i:GDN_CONV - i, :]
        qkv_all = _silu(conv)
        for u in range(n_sub):
            qkv = qkv_all[u * c:(u + 1) * c]
            ab = ab_ref[s, u * c:(u + 1) * c, :]
            x = ab + dtb_ref[...]
            sp = jnp.maximum(x, 0.0) + jnp.log1p(jnp.exp(-jnp.abs(x)))
            g_all = jnp.where(valid, -jnp.exp(alog_ref[...]) * sp, 0.0)
            beta_all = jnp.where(valid, 1.0 / (1.0 + jnp.exp(-ab)), 0.0)
            gc_all = _cumsum_rows(g_all)
            heads = []
            for h in range(H_GDN):
                q = qkv[:, h * HEAD_DIM:(h + 1) * HEAD_DIM]
                k = qkv[:, W_GDN + h * HEAD_DIM:W_GDN + (h + 1) * HEAD_DIM]
                v = qkv[:, 2 * W_GDN + h * HEAD_DIM:2 * W_GDN + (h + 1) * HEAD_DIM]
                q = q * lax.rsqrt(jnp.sum(q * q, axis=-1, keepdims=True) + 1e-6) * HEAD_DIM ** -0.5
                k = k * lax.rsqrt(jnp.sum(k * k, axis=-1, keepdims=True) + 1e-6)
                beta = beta_all[:, H_GDN + h:H_GDN + h + 1]
                gc = gc_all[:, h:h + 1]
                g_last = gc_all[c - 1:c, h:h + 1]
                heads.append(dict(q=q, k=k, kb=k * beta, vb=v * beta, gc=gc, g_last=g_last))
            prep[(s, u)] = heads

    n = MXU_DIM
    row = lax.broadcasted_iota(jnp.int32, (n, n), 0)
    col = lax.broadcasted_iota(jnp.int32, (n, n), 1)
    same = row // c == col // c
    lower = same & (row >= col)
    strict = same & (row > col)

    groups = []
    for g in range(len(units) // per_group):
        pieces = [(su, h) for su in units[g * per_group:(g + 1) * per_group] for h in range(H_GDN)]
        cat = lambda key: jnp.concatenate([prep[su][h][key] for su, h in pieces], axis=0)
        k_all, kb_all, gcol = cat("k"), cat("kb"), cat("gc")
        grow = jnp.broadcast_to(gcol, (n, LANES)).T[0:1, :]
        decay = jnp.where(lower, jnp.exp(jnp.where(lower, gcol - grow, 0.0)), 0.0)
        lmat = jnp.where(strict, _dot_nt(kb_all, k_all) * decay, 0.0)
        tinv = _tri_inverse(lmat, c)
        rhs = jnp.concatenate([cat("vb"), kb_all * jnp.exp(gcol)], axis=1)
        uw = _dot(tinv, rhs)
        qk = _dot_nt(cat("q"), k_all) * decay
        groups.append((pieces, uw, qk))

    for pieces, uw, qk in groups:
        v_new = []
        for j, ((s, u), h) in enumerate(pieces):
            blk = uw[j * c:(j + 1) * c]
            v_new.append(blk[:, :HEAD_DIM] - _dot(blk[:, HEAD_DIM:], s_scr[s, h]))
        o_intra = _dot(qk, jnp.concatenate(v_new, axis=0))
        for j, ((s, u), h) in enumerate(pieces):
            p = prep[(s, u)][h]
            st = s_scr[s, h]
            o = _dot(p["q"] * jnp.exp(p["gc"]), st) + o_intra[j * c:(j + 1) * c]
            s_scr[s, h] = st * jnp.exp(p["g_last"]) + _dot_tn(p["k"] * jnp.exp(p["g_last"] - p["gc"]), v_new[j])
            o = o * lax.rsqrt(jnp.mean(o * o, axis=-1, keepdims=True) + RMS_EPS) * nw
            sl = slice(h * HEAD_DIM, (h + 1) * HEAD_DIM)
            o_ref[s, u * c:(u + 1) * c, sl] = o * _silu(z_ref[s, u * c:(u + 1) * c, sl])

    @pl.when(c_idx == pl.num_programs(1) - 1)
    def _():
        sfin_ref[...] = s_scr[...]


def gdn_mixer(qkv, z, ab, hist, s0, conv_w, a_log, dt_bias, norm_w, chunk, n_valid, n_seq, n_sub):
    s_n, t, _ = qkv[0].shape
    rows = n_sub * chunk
    assert s_n % n_seq == 0 and t % rows == 0 and (n_seq * n_sub * H_GDN * chunk) % MXU_DIM == 0
    lane_pad = lambda a: jnp.zeros((1, LANES), f32).at[0, :H_GDN].set(a)
    return pl.pallas_call(
        functools.partial(_gdn_kernel, chunk=chunk, n_valid=n_valid),
        grid=(s_n // n_seq, t // rows),
        in_specs=[pl.BlockSpec((n_seq, rows, 3 * W_GDN), lambda s, c: (s, c, qkv[1])),
                  pl.BlockSpec((n_seq, rows, W_GDN), lambda s, c: (s, c, z[1])),
                  pl.BlockSpec((n_seq, rows, LANES), lambda s, c: (s, c, ab[1])),
                  pl.BlockSpec((n_seq, SUBLANES, 3 * W_GDN), lambda s, c: (s, 0, 0)),
                  pl.BlockSpec((n_seq, H_GDN, HEAD_DIM, HEAD_DIM), lambda s, c: (s, 0, 0, 0)),
                  pl.BlockSpec((GDN_CONV, 3 * W_GDN), lambda s, c: (0, 0)),
                  pl.BlockSpec((1, LANES), lambda s, c: (0, 0)),
                  pl.BlockSpec((1, LANES), lambda s, c: (0, 0)),
                  pl.BlockSpec((1, HEAD_DIM), lambda s, c: (0, 0))],
        out_specs=[pl.BlockSpec((n_seq, rows, W_GDN), lambda s, c: (s, c, 0)),
                   pl.BlockSpec((n_seq, H_GDN, HEAD_DIM, HEAD_DIM), lambda s, c: (s, 0, 0, 0))],
        out_shape=[jax.ShapeDtypeStruct((s_n, t, W_GDN), f32),
                   jax.ShapeDtypeStruct((s_n, H_GDN, HEAD_DIM, HEAD_DIM), f32)],
        scratch_shapes=[pltpu.VMEM((n_seq, H_GDN, HEAD_DIM, HEAD_DIM), f32),
                        pltpu.VMEM((n_seq, SUBLANES, 3 * W_GDN), f32)],
        compiler_params=_cparams("arbitrary", "arbitrary"),
        name="gdn_mixer",
    )(qkv[0], z[0], ab[0], hist, s0, conv_w, lane_pad(a_log), lane_pad(dt_bias), norm_w.reshape(1, HEAD_DIM))


def _ret_kernel(q_ref, k_ref, v_ref, g_ref, cs_ref, sn_ref, s0_ref, o_ref, sfin_ref, s_scr, *, n_valid):
    c_idx = pl.program_id(1)
    n_seq, c, _ = q_ref.shape
    c_len = min(c, n_valid)

    @pl.when(c_idx == 0)
    def _():
        s_scr[...] = s0_ref[...]

    cs = cs_ref[...]
    sn = sn_ref[...]
    rot = lambda x: x * cs + pltpu.roll(x, HEAD_DIM // 2, 1) * sn
    r2 = lax.broadcasted_iota(jnp.int32, (c, c), 0)
    c2 = lax.broadcasted_iota(jnp.int32, (c, c), 1)
    rel = (r2 - c2).astype(f32)
    idx = lax.broadcasted_iota(jnp.int32, (c, 1), 0)
    idx_f = idx.astype(f32)
    valid = idx < n_valid

    for h in range(H_RET):
        lg = math.log(1.0 - 2.0 ** (-5.0 - h))
        sl = slice(h * HEAD_DIM, (h + 1) * HEAD_DIM)
        dmat = jnp.where(rel >= 0, jnp.exp(jnp.maximum(rel, 0.0) * lg), 0.0)
        xi = jnp.exp((idx_f + 1.0) * lg)
        zeta = jnp.where(valid, jnp.exp((c_len - 1.0 - idx_f) * lg), 0.0)
        for s in range(n_seq):
            q = rot(q_ref[s, :, sl])
            k = rot(k_ref[s, :, sl]) * HEAD_DIM ** -0.5
            v = v_ref[s, :, sl]
            st = s_scr[s, h]
            o = _dot(_dot_nt(q, k) * dmat, v) + _dot(q * xi, st)
            s_scr[s, h] = math.exp(c_len * lg) * st + _dot_tn(k * zeta, v)
            o = o * lax.rsqrt(jnp.mean(o * o, axis=-1, keepdims=True) + RMS_EPS)
            o_ref[s, :, sl] = o * _silu(g_ref[s, :, sl])

    @pl.when(c_idx == pl.num_programs(1) - 1)
    def _():
        sfin_ref[...] = s_scr[...]


def retention_mixer(qkvg, cs, sn, s0, chunk, n_valid, n_seq):
    proj, base = qkvg
    s_n, t, _ = proj.shape
    assert s_n % n_seq == 0 and t % chunk == 0
    col_spec = lambda part: pl.BlockSpec((n_seq, chunk, W_RET), lambda s, c: (s, c, base + part))
    return pl.pallas_call(
        functools.partial(_ret_kernel, n_valid=n_valid),
        grid=(s_n // n_seq, t // chunk),
        in_specs=[col_spec(0), col_spec(1), col_spec(2), col_spec(3),
                  pl.BlockSpec((chunk, HEAD_DIM), lambda s, c: (c, 0)),
                  pl.BlockSpec((chunk, HEAD_DIM), lambda s, c: (c, 0)),
                  pl.BlockSpec((n_seq, H_RET, HEAD_DIM, HEAD_DIM), lambda s, c: (s, 0, 0, 0))],
        out_specs=[pl.BlockSpec((n_seq, chunk, W_RET), lambda s, c: (s, c, 0)),
                   pl.BlockSpec((n_seq, H_RET, HEAD_DIM, HEAD_DIM), lambda s, c: (s, 0, 0, 0))],
        out_shape=[jax.ShapeDtypeStruct((s_n, t, W_RET), f32),
                   jax.ShapeDtypeStruct((s_n, H_RET, HEAD_DIM, HEAD_DIM), f32)],
        scratch_shapes=[pltpu.VMEM((n_seq, H_RET, HEAD_DIM, HEAD_DIM), f32)],
        compiler_params=_cparams("arbitrary", "arbitrary"),
        name="retention_mixer",
    )(proj, proj, proj, proj, cs, sn, s0)


def _rotary_tables(pos):
    half = HEAD_DIM // 2
    inv = 1.0 / (10000.0 ** jnp.linspace(0.0, 1.0, half, dtype=f32))
    ang = pos.astype(f32)[:, None] * inv[None, :]
    cos, sin = jnp.cos(ang), jnp.sin(ang)
    return jnp.concatenate([cos, cos], axis=1), jnp.concatenate([-sin, sin], axis=1)


def _t5_bias_shifted(dist, tab):
    val = tab(0)
    for b in range(1, NUM_BUCKETS):
        val = jnp.where(dist >= _BUCKET_START[b], tab(b), val)
    return val - tab(NUM_BUCKETS - 1)


def _top_blocks(gate, n_sel):
    lane = lax.broadcasted_iota(jnp.int32, gate.shape, 1)
    sel = jnp.zeros(gate.shape, f32)
    for _ in range(n_sel):
        m = jnp.max(gate, axis=1, keepdims=True)
        first = jnp.min(jnp.where(gate == m, lane, gate.shape[1]), axis=1, keepdims=True)
        pick = (lane == first) & (m > 0.5 * NEG_INF)
        sel = jnp.where(pick, 1.0, sel)
        gate = jnp.where(pick, NEG_INF, gate)
    return sel


def _scores(q_aug, k_aug):
    return lax.dot_general(q_aug, k_aug, (((1,), (1,)), ((), ())), preferred_element_type=f32)


def _moba_prompt_kernel(tab_ref, q_ref, k_ref, v_ref, o_ref,
                        kaug_scr, v_scr, kmean_scr, bias_scr, s_scr, l_scr, acc_scr):
    h = pl.program_id(0)
    i = pl.program_id(1)
    blk = MOBA_BLOCK
    t = k_ref.shape[0]
    n_blocks = t // blk
    ck = FAR_CHUNK * blk
    masked_lane = LANES - 1

    @pl.when(i == 0)
    def _():
        k = k_ref[...]
        kmean_scr[...] = jnp.zeros(kmean_scr.shape, f32)
        kmean_scr[0:n_blocks, :] = jnp.mean(k.reshape(n_blocks, blk, HEAD_DIM), axis=1)
        row = lax.broadcasted_iota(jnp.int32, (t, LANES), 0)
        lane_t = lax.broadcasted_iota(jnp.int32, (t, LANES), 1)
        kaug_scr[blk:blk + t, 0:HEAD_DIM] = k.astype(bf16)
        kaug_scr[blk:blk + t, HEAD_DIM:] = jnp.where(lane_t == row // blk, 1.0, 0.0).astype(bf16)
        v_scr[blk:blk + t, :] = v_ref[...].astype(bf16)
        for start, size in ((0, blk), (blk + t, kaug_scr.shape[0] - blk - t)):
            if not size:
                continue
            lane_p = lax.broadcasted_iota(jnp.int32, (size, LANES), 1)
            kaug_scr[start:start + size, 0:HEAD_DIM] = jnp.zeros((size, HEAD_DIM), bf16)
            kaug_scr[start:start + size, HEAD_DIM:] = jnp.where(lane_p == masked_lane, 1.0, 0.0).astype(bf16)
            v_scr[start:start + size, :] = jnp.zeros((size, HEAD_DIM), bf16)
        r2 = lax.broadcasted_iota(jnp.int32, (blk, blk), 0)
        c2 = lax.broadcasted_iota(jnp.int32, (blk, blk), 1)
        tab = lambda b: tab_ref[b, h]
        bias_scr[0] = _t5_bias_shifted(r2 - c2 + blk, tab)
        bias_scr[1] = jnp.where(r2 >= c2, _t5_bias_shifted(r2 - c2, tab), NEG_INF)

    q = q_ref[...]
    lane = lax.broadcasted_iota(jnp.int32, (blk, LANES), 1)
    gate = jnp.where(lane < i, _dot3_nt(q, kmean_scr[...]), NEG_INF)
    sel = _top_blocks(gate, MOBA_TOPK) > 0.5
    qs = q * HEAD_DIM ** -0.5
    neg_far = jnp.where(sel & (lane < i - 1), 0.0, NEG_INF)
    neg_near = jnp.where((sel & (lane == i - 1)) | (lane == i), 0.0, NEG_INF)
    q_far = jnp.concatenate([qs, neg_far], axis=1).astype(bf16)
    q_near = jnp.concatenate([qs, neg_near], axis=1).astype(bf16)
    n_chunks = (jnp.maximum(i - 1, 0) + FAR_CHUNK - 1) // FAR_CHUNK

    far_rows = lambda c: pl.ds(pl.multiple_of(blk + c * ck, blk), ck)

    def scores_pass(c, mx):
        s = _scores(q_far, kaug_scr[far_rows(c), :])
        s_scr[c] = s
        return jnp.maximum(mx, _fold(s, jnp.maximum))

    mx = lax.fori_loop(0, n_chunks, scores_pass, jnp.full((blk, LANES), NEG_INF, f32))
    near_rows = pl.ds(pl.multiple_of(i * blk, blk), 2 * blk)
    s_near = _scores(q_near, kaug_scr[near_rows, :]) + jnp.concatenate([bias_scr[0], bias_scr[1]], axis=1)
    m = jnp.max(jnp.maximum(mx, _fold(s_near, jnp.maximum)), axis=1, keepdims=True)
    p_near = jnp.exp(s_near - m)
    l_scr[...] = _fold(p_near, jnp.add)
    acc_scr[...] = jnp.dot(p_near.astype(bf16), v_scr[near_rows, :], preferred_element_type=f32)

    def values_pass(c, carry):
        p = jnp.exp(s_scr[c] - m)
        l_scr[...] += _fold(p, jnp.add)
        acc_scr[...] += jnp.dot(p.astype(bf16), v_scr[far_rows(c), :], preferred_element_type=f32)
        return carry

    lax.fori_loop(0, n_chunks, values_pass, 0)
    o_ref[...] = acc_scr[...] / jnp.sum(l_scr[...], axis=1, keepdims=True)


def moba_prompt(q, k, v, rel_bias):
    t = q[0].shape[0]
    blk = MOBA_BLOCK
    n_blocks = t // blk
    assert t % blk == 0 and n_blocks < LANES - 1 and blk >= _BUCKET_START[-1]
    qb, kb, vb = q[1], k[1], v[1]
    far_max = -(-max(n_blocks - 2, 1) // FAR_CHUNK) * FAR_CHUNK
    kv_rows = (1 + max(n_blocks, far_max + 1)) * blk
    return pl.pallas_call(
        _moba_prompt_kernel,
        grid=(H_MOBA, n_blocks),
        in_specs=[pl.BlockSpec(memory_space=pltpu.SMEM),
                  pl.BlockSpec((blk, HEAD_DIM), lambda h, i: (i, qb + h)),
                  pl.BlockSpec((t, HEAD_DIM), lambda h, i: (0, kb + h)),
                  pl.BlockSpec((t, HEAD_DIM), lambda h, i: (0, vb + h))],
        out_specs=pl.BlockSpec((blk, HEAD_DIM), lambda h, i: (i, h)),
        out_shape=jax.ShapeDtypeStruct((t, W_MOBA), f32),
        scratch_shapes=[pltpu.VMEM((kv_rows, 2 * HEAD_DIM), bf16),
                        pltpu.VMEM((kv_rows, HEAD_DIM), bf16),
                        pltpu.VMEM((LANES, HEAD_DIM), f32),
                        pltpu.VMEM((2, blk, blk), f32),
                        pltpu.VMEM((far_max // FAR_CHUNK, blk, FAR_CHUNK * blk), f32),
                        pltpu.VMEM((blk, LANES), f32),
                        pltpu.VMEM((blk, HEAD_DIM), f32)],
        compiler_params=_cparams("arbitrary", "arbitrary"),
        name="moba_prompt",
    )(rel_bias, q[0], k[0], v[0])


def _moba_sample_kernel(pt_ref, q_ref, kn_ref, vn_ref, tabr_ref, *rest, n_pages, n_new):
    kp = rest[:n_pages]
    vp = rest[n_pages:2 * n_pages]
    o_ref, onehot_scr, near_scr, own_scr, s_scr = rest[2 * n_pages:]
    page = kp[0].shape[0]
    pr = page * H_MOBA
    past = n_pages * page
    ppb = MOBA_BLOCK // page
    n_blocks = n_pages // ppb
    rows = H_MOBA * n_new
    n_own = kn_ref.shape[0]
    near = [p for p in range(n_pages) if past - (p * page + page - 1) < _BUCKET_START[-1]]

    r_l = lax.broadcasted_iota(jnp.int32, (rows, LANES), 0)
    l_l = lax.broadcasted_iota(jnp.int32, (rows, LANES), 1)
    head_r = r_l // n_new
    t_r = r_l % n_new

    @pl.when(pl.program_id(0) == 0)
    def _():
        kr = lax.broadcasted_iota(jnp.int32, (pr, LANES), 0)
        kl = lax.broadcasted_iota(jnp.int32, (pr, LANES), 1)
        for n in range(n_blocks):
            onehot_scr[n] = jnp.where(kl == n * H_MOBA + kr % H_MOBA, 1.0, 0.0).astype(bf16)
        tabr = tabr_ref[...]
        tab = lambda b: tabr[:, b:b + 1]
        rr = lax.broadcasted_iota(jnp.int32, (rows, pr), 0)
        cc = lax.broadcasted_iota(jnp.int32, (rows, pr), 1)
        for idx, p in enumerate(near):
            near_scr[idx] = _t5_bias_shifted(past + rr % n_new - (p * page + cc // H_MOBA), tab)
        t_k = l_l // H_MOBA
        visible = (l_l % H_MOBA == head_r) & (t_k <= t_r) & (t_k < n_new)
        own_scr[...] = jnp.where(visible, _t5_bias_shifted(t_r - t_k, tab), NEG_INF)

    q = q_ref[...]
    kmean = []
    for n in range(n_blocks):
        tot = functools.reduce(jnp.add, [jnp.sum(kp[p][...], axis=0) for p in range(n * ppb, (n + 1) * ppb)])
        kmean.append(tot * (1.0 / MOBA_BLOCK))
    kmean.append(jnp.zeros((LANES - n_blocks * H_MOBA, HEAD_DIM), f32))
    gate = _dot3_nt(q, jnp.concatenate(kmean, axis=0))
    own_head = (l_l < n_blocks * H_MOBA) & (l_l % H_MOBA == head_r)
    sel = _top_blocks(jnp.where(own_head, gate, NEG_INF), min(MOBA_TOPK, n_blocks))
    qs = q * HEAD_DIM ** -0.5
    q_aug = jnp.concatenate([qs, jnp.where(sel > 0.5, 0.0, NEG_INF)], axis=1).astype(bf16)

    mx = jnp.full((rows, LANES), NEG_INF, f32)
    for p in range(n_pages):
        k_aug = jnp.concatenate([kp[p][...].reshape(pr, HEAD_DIM).astype(bf16), onehot_scr[p // ppb]], axis=1)
        s = _scores(q_aug, k_aug)
        if p in near:
            s = s + near_scr[near.index(p)]
        s_scr[:, p * pr:(p + 1) * pr] = s
        mx = jnp.maximum(mx, _fold(s, jnp.maximum))
    pad_own = lambda r: jnp.concatenate([r[...], jnp.zeros((LANES - n_own, HEAD_DIM), f32)], axis=0)
    s_own = _dot_nt(qs, pad_own(kn_ref)) + own_scr[...]
    m = jnp.max(jnp.maximum(mx, s_own), axis=1, keepdims=True)
    p_own = jnp.exp(s_own - m)
    lsum = p_own
    acc = _dot(p_own, pad_own(vn_ref))
    for p in range(n_pages):
        pe = jnp.exp(s_scr[:, p * pr:(p + 1) * pr] - m)
        lsum = lsum + _fold(pe, jnp.add)
        acc = acc + _dot(pe, vp[p][...].reshape(pr, HEAD_DIM))
    o_ref[...] = acc / jnp.sum(lsum, axis=1, keepdims=True)


def moba_sample(q, k_new, v_new, cache_k, cache_v, page_table, rel_bias, layer, n_new):
    bsz, rows, _ = q.shape
    n_pages = page_table.shape[1]
    page = cache_k.shape[2]
    past = n_pages * page
    n_blocks = past // MOBA_BLOCK
    assert past % MOBA_BLOCK == 0 and MOBA_BLOCK % page == 0 and n_blocks * H_MOBA <= LANES
    assert rows == H_MOBA * n_new and k_new.shape[1] <= LANES
    n_near = len([p for p in range(n_pages) if past - (p * page + page - 1) < _BUCKET_START[-1]])
    tab_rows = jnp.pad(jnp.repeat(rel_bias.T, n_new, axis=0), ((0, 0), (0, LANES - NUM_BUCKETS)))
    page_spec = lambda p: pl.BlockSpec((None, None, page, H_MOBA, HEAD_DIM),
                                       lambda b, pt: (layer, pt[b, p], 0, 0, 0))
    row_spec = lambda a: pl.BlockSpec((None,) + a.shape[1:], lambda b, pt: (b, 0, 0))
    grid_spec = pltpu.PrefetchScalarGridSpec(
        num_scalar_prefetch=1,
        grid=(bsz,),
        in_specs=([row_spec(q), row_spec(k_new), row_spec(v_new),
                   pl.BlockSpec(tab_rows.shape, lambda b, pt: (0, 0))]
                  + [page_spec(p) for p in range(n_pages)] * 2),
        out_specs=row_spec(q),
        scratch_shapes=[pltpu.VMEM((n_blocks, page * H_MOBA, LANES), bf16),
                        pltpu.VMEM((max(n_near, 1), rows, page * H_MOBA), f32),
                        pltpu.VMEM((rows, LANES), f32),
                        pltpu.VMEM((rows, past * H_MOBA), f32)],
    )
    return pl.pallas_call(
        functools.partial(_moba_sample_kernel, n_pages=n_pages, n_new=n_new),
        grid_spec=grid_spec,
        out_shape=jax.ShapeDtypeStruct(q.shape, f32),
        compiler_params=_cparams("arbitrary"),
        name="moba_sample",
    )(page_table, q, k_new, v_new, tab_rows, *([cache_k] * n_pages), *([cache_v] * n_pages))


def _post(x, y, gain, gate):
    return x + gate * (y * lax.rsqrt(jnp.mean(y * y, axis=-1, keepdims=True) + RMS_EPS) * gain)


def _out_proj_kernel(oa_ref, ob_ref, oc_ref, w_ref, x_ref, gt_ref, g_ref, o_ref):
    y = (jnp.dot(oa_ref[...].astype(bf16), w_ref[0:W_GDN, :], preferred_element_type=f32)
         + jnp.dot(ob_ref[...].astype(bf16), w_ref[W_GDN:W_GDN + W_MOBA, :], preferred_element_type=f32)
         + jnp.dot(oc_ref[...].astype(bf16), w_ref[W_GDN + W_MOBA:, :], preferred_element_type=f32))
    o_ref[...] = _post(x_ref[...], y, g_ref[...], gt_ref[...])


def out_proj_residual(o_a, o_b, o_c, w_out, x, mod, gt_blk, gain, tm):
    m, d = x.shape
    per_row = mod.shape[0] != 1
    mod_spec = pl.BlockSpec((tm if per_row else 1, d), (lambda i: (i, gt_blk)) if per_row else (lambda i: (0, gt_blk)))
    return pl.pallas_call(
        _out_proj_kernel,
        grid=(m // tm,),
        in_specs=[pl.BlockSpec((tm, W_GDN), lambda i: (i, 0)),
                  pl.BlockSpec((tm, W_MOBA), lambda i: (i, 0)),
                  pl.BlockSpec((tm, W_RET), lambda i: (i, 0)),
                  pl.BlockSpec(w_out.shape, lambda i: (0, 0)),
                  pl.BlockSpec((tm, d), lambda i: (i, 0)),
                  mod_spec,
                  pl.BlockSpec((1, d), lambda i: (0, 0))],
        out_specs=pl.BlockSpec((tm, d), lambda i: (i, 0)),
        out_shape=jax.ShapeDtypeStruct((m, d), f32),
        compiler_params=_cparams("arbitrary"),
        name="out_proj_residual",
    )(o_a, o_b, o_c, w_out, x, mod, gain.reshape(1, d))


def _ffn_down_kernel(ug_ref, uu_ref, pg_ref, pu_ref, hg_ref, hu_ref, cwg_ref, cwu_ref, wd_ref,
                     x_ref, gt_ref, g_ref, o_ref, acc_ref, *, row_shift):
    i = pl.program_id(0)
    f = pl.program_id(1)
    hdr = hg_ref.shape[0]
    first = i == 0

    def conv(u_ref, p_ref, h_ref, cw_ref):
        head = jnp.where(first, h_ref[...], p_ref[...])
        ext = jnp.concatenate([head, u_ref[...]], axis=0)
        out = ext[hdr:] * cw_ref[FFN_CONV - 1:FFN_CONV, :]
        for j in range(1, FFN_CONV):
            out = out + pltpu.roll(ext, j * row_shift, 0)[hdr:] * cw_ref[FFN_CONV - 1 - j:FFN_CONV - j, :]
        return out

    act = _silu(conv(ug_ref, pg_ref, hg_ref, cwg_ref)) * conv(uu_ref, pu_ref, hu_ref, cwu_ref)
    part = jnp.dot(act.astype(bf16), wd_ref[...], preferred_element_type=f32)

    @pl.when(f == 0)
    def _():
        acc_ref[...] = part

    @pl.when(f > 0)
    def _():
        acc_ref[...] += part

    @pl.when(f == pl.num_programs(1) - 1)
    def _():
        o_ref[...] = _post(x_ref[...], acc_ref[...], g_ref[...], gt_ref[...])


def ffn_down_residual(u, hist, conv_w, w_down, x, mod, gt_blk, gain, tm, tf, row_shift):
    m, d = x.shape
    ff = w_down.shape[0]
    nf = ff // tf
    hdr = hist.shape[0]
    assert hdr >= (FFN_CONV - 1) * row_shift and tm % hdr == 0
    per_row = mod.shape[0] != 1
    mod_spec = pl.BlockSpec((tm if per_row else 1, d),
                            (lambda i, f: (i, gt_blk)) if per_row else (lambda i, f: (0, gt_blk)))
    prev = lambda off: (lambda i, f: (jnp.maximum(i * (tm // hdr) - 1, 0), f + off))
    return pl.pallas_call(
        functools.partial(_ffn_down_kernel, row_shift=row_shift),
        grid=(m // tm, nf),
        in_specs=[pl.BlockSpec((tm, tf), lambda i, f: (i, f)),
                  pl.BlockSpec((tm, tf), lambda i, f: (i, f + nf)),
                  pl.BlockSpec((hdr, tf), prev(0)),
                  pl.BlockSpec((hdr, tf), prev(nf)),
                  pl.BlockSpec((hdr, tf), lambda i, f: (0, f)),
                  pl.BlockSpec((hdr, tf), lambda i, f: (0, f + nf)),
                  pl.BlockSpec((FFN_CONV, tf), lambda i, f: (0, f)),
                  pl.BlockSpec((FFN_CONV, tf), lambda i, f: (0, f + nf)),
                  pl.BlockSpec((tf, d), lambda i, f: (f, 0)),
                  pl.BlockSpec((tm, d), lambda i, f: (i, 0)),
                  mod_spec,
                  pl.BlockSpec((1, d), lambda i, f: (0, 0))],
        out_specs=pl.BlockSpec((tm, d), lambda i, f: (i, 0)),
        out_shape=jax.ShapeDtypeStruct((m, d), f32),
        scratch_shapes=[pltpu.VMEM((tm, d), f32)],
        compiler_params=_cparams("arbitrary", "arbitrary"),
        name="ffn_down_residual",
    )(u, u, u, u, hist, hist, conv_w, conv_w, w_down, x, mod, gain.reshape(1, d))


def _permute_w_in(w_in):
    s1 = 3 * W_GDN
    s2 = s1 + W_GDN
    s4 = s2 + 2 * H_GDN
    ab = w_in[..., s2:s4]
    pad = jnp.zeros(w_in.shape[:-1] + (PROJ_W - COL_AB - 2 * H_GDN,), w_in.dtype)
    return jnp.concatenate([w_in[..., :s2], w_in[..., s4:], ab, pad], axis=-1).astype(bf16)


def kernel(x_prompt, x_sample, c_prompt, c_sample, cache_k, cache_v, page_table, state_gdn_conv, state_gdn, state_ret, state_ffn_conv, w_ada, b_ada, g_pre_mix, g_post_mix, g_pre_ffn, g_post_ffn, w_in, gdn_conv_w, gdn_a_log, gdn_dt_bias, gdn_norm_w, rel_bias, w_out, w_up, ffn_conv_w, w_down):
    bp, seq, d = x_prompt.shape
    bs, dec = x_sample.shape[:2]
    depth = w_in.shape[0]
    assert bp == 1 and dec <= SUBLANES
    past_len = page_table.shape[1] * cache_k.shape[2]
    d_ff = w_down.shape[1]
    pad_t = SUBLANES - dec
    seq_per_step = MXU_DIM // (H_GDN * SUBLANES)

    w_in_b = _permute_w_in(w_in)
    w_out_b = w_out.astype(bf16)
    w_up_b = w_up.astype(bf16)
    w_down_b = w_down.astype(bf16)

    n_c = bp + bs
    c_all = jnp.concatenate([c_prompt, c_sample, jnp.zeros((-n_c % SUBLANES, d), f32)], axis=0)
    mod = ada_modulation(c_all, w_ada, b_ada)

    cs_p, sn_p = _rotary_tables(jnp.arange(seq, dtype=jnp.int32))
    cs_s, sn_s = _rotary_tables(past_len + jnp.arange(SUBLANES, dtype=jnp.int32))

    yp = x_prompt.reshape(seq, d)
    ys = jnp.swapaxes(x_sample, 0, 1).reshape(dec * bs, d)
    to_bm = lambda a: jnp.pad(jnp.swapaxes(a.reshape(dec, bs, -1), 0, 1), ((0, 0), (0, pad_t), (0, 0)))
    to_tm = lambda a: jnp.swapaxes(a[:, :dec], 0, 1).reshape(dec * bs, -1)
    heads_tm = lambda a: a.reshape(dec, bs, H_MOBA, HEAD_DIM)

    outs = [[] for _ in range(12)]
    for l in range(depth):
        mod_p = mod[l, 0:bp]
        mod_s = jnp.tile(mod[l, bp:bp + bs], (dec, 1))

        qkva_p, z_p, qb_p, kb_p, vb_p, c_p, ab_p = norm_mod_matmul_split(
            yp, g_pre_mix[l], mod_p, 1, 0, w_in_b[l], tm=512, tn=SPLIT_TILE,
            widths=(3 * W_GDN, W_GDN, W_MOBA, W_MOBA, W_MOBA, 4 * W_RET, PROJ_W - COL_AB))
        oa_p, gs_p = gdn_mixer((qkva_p[None], 0), (z_p[None], 0), (ab_p[None], 0),
                               jnp.zeros((1, SUBLANES, 3 * W_GDN), f32),
                               jnp.zeros((1, H_GDN, HEAD_DIM, HEAD_DIM), f32),
                               gdn_conv_w[l], gdn_a_log[l], gdn_dt_bias[l], gdn_norm_w[l],
                               chunk=GDN_CHUNK, n_valid=GDN_CHUNK, n_seq=1, n_sub=2)
        oc_p, rs_p = retention_mixer((c_p[None], 0), cs_p, sn_p, jnp.zeros((1, H_RET, HEAD_DIM, HEAD_DIM), f32),
                                     chunk=RET_CHUNK, n_valid=RET_CHUNK, n_seq=1)
        ob_p = moba_prompt((qb_p, 0), (kb_p, 0), (vb_p, 0), rel_bias)
        yp = out_proj_residual(oa_p[0], ob_p, oc_p[0], w_out_b[l], yp, mod_p, 2, g_post_mix[l], tm=512)

        proj_s = norm_mod_matmul(ys, g_pre_mix[l], mod_s, 1, 0, w_in_b[l], tm=dec * bs, tn=PROJ_TILE)
        proj_sb = to_bm(proj_s)
        hist_s = jnp.pad(state_gdn_conv[l], ((0, 0), (SUBLANES - (GDN_CONV - 1), 0), (0, 0)))
        oa_s, gs_s = gdn_mixer((proj_sb, COL_QKV_A // (3 * W_GDN)), (proj_sb, COL_Z_A // W_GDN),
                               (proj_sb, COL_AB // LANES), hist_s, state_gdn[l], gdn_conv_w[l], gdn_a_log[l],
                               gdn_dt_bias[l], gdn_norm_w[l], chunk=SUBLANES, n_valid=dec, n_seq=seq_per_step,
                               n_sub=1)
        oc_s, rs_s = retention_mixer((proj_sb, COL_QKVG_C // W_RET), cs_s, sn_s, state_ret[l], chunk=SUBLANES,
                                     n_valid=dec, n_seq=seq_per_step)
        q_s, k_s, v_s = [heads_tm(proj_s[:, COL_QKV_B + j * W_MOBA:COL_QKV_B + (j + 1) * W_MOBA]) for j in range(3)]
        k_s, v_s = jnp.swapaxes(k_s, 0, 1), jnp.swapaxes(v_s, 0, 1)
        ob_s = moba_sample(jnp.transpose(q_s, (1, 2, 0, 3)).reshape(bs, H_MOBA * dec, HEAD_DIM),
                           k_s.reshape(bs, dec * H_MOBA, HEAD_DIM), v_s.reshape(bs, dec * H_MOBA, HEAD_DIM),
                           cache_k, cache_v, page_table, rel_bias, l, dec)
        ob_s = jnp.transpose(ob_s.reshape(bs, H_MOBA, dec, HEAD_DIM), (2, 0, 1, 3)).reshape(dec * bs, W_MOBA)
        ys = out_proj_residual(to_tm(oa_s), ob_s, to_tm(oc_s), w_out_b[l], ys, mod_s, 2,
                               g_post_mix[l], tm=dec * bs)

        u_p = norm_mod_matmul(yp, g_pre_ffn[l], mod_p, 4, 3, w_up_b[l], tm=1024, tn=1024)
        yp = ffn_down_residual(u_p, jnp.zeros((SUBLANES, 2 * d_ff), f32), ffn_conv_w[l], w_down_b[l], yp,
                               mod_p, 5, g_post_ffn[l], tm=512, tf=d_ff // 4, row_shift=1)
        u_s = norm_mod_matmul(ys, g_pre_ffn[l], mod_s, 4, 3, w_up_b[l], tm=dec * bs, tn=1024)
        hist_f = jnp.swapaxes(state_ffn_conv[l], 0, 1).reshape((FFN_CONV - 1) * bs, 2 * d_ff)
        ys = ffn_down_residual(u_s, hist_f, ffn_conv_w[l], w_down_b[l], ys, mod_s, 5, g_post_ffn[l],
                               tm=dec * bs, tf=512, row_shift=bs)

        outs[0].append(kb_p.reshape(bp, seq, H_MOBA, HEAD_DIM))
        outs[1].append(vb_p.reshape(bp, seq, H_MOBA, HEAD_DIM))
        outs[2].append(k_s)
        outs[3].append(v_s)
        outs[4].append(qkva_p[seq - (GDN_CONV - 1):].reshape(bp, GDN_CONV - 1, 3 * W_GDN))
        outs[5].append(proj_sb[:, dec - (GDN_CONV - 1):dec, :3 * W_GDN])
        outs[6].append(gs_p)
        outs[7].append(gs_s)
        outs[8].append(rs_p)
        outs[9].append(rs_s)
        outs[10].append(u_p[seq - (FFN_CONV - 1):].reshape(bp, FFN_CONV - 1, 2 * d_ff))
        outs[11].append(jnp.swapaxes(u_s.reshape(dec, bs, 2 * d_ff)[dec - (FFN_CONV - 1):], 0, 1))

    y_prompt = yp.reshape(bp, seq, d)
    y_sample = jnp.swapaxes(ys.reshape(dec, bs, d), 0, 1)
    return (y_prompt, y_sample) + tuple(jnp.stack(o, axis=0) for o in outs)
```

```python
import functools
import math

import numpy as np
import jax
import jax.numpy as jnp
from jax import lax
from jax.experimental import pallas as pl
from jax.experimental.pallas import tpu as pltpu

f32 = jnp.float32
bf16 = jnp.bfloat16

HEAD_DIM = 128
H_GDN = 4
H_MOBA = 8
H_RET = 4
W_GDN = H_GDN * HEAD_DIM
W_MOBA = H_MOBA * HEAD_DIM
W_RET = H_RET * HEAD_DIM
GDN_CONV = 4
FFN_CONV = 3
MOBA_BLOCK = 256
MOBA_TOPK = 3
NUM_BUCKETS = 32
MAX_DISTANCE = 128
RMS_EPS = 1e-6
NEG_INF = -1e30

SUBLANES = 8
LANES = 128
MXU_DIM = 256
VMEM_LIMIT = 56 * 1024 * 1024

COL_QKV_A = 0
COL_Z_A = 3 * W_GDN
COL_QKV_B = COL_Z_A + W_GDN
COL_QKVG_C = COL_QKV_B + 3 * W_MOBA
COL_AB = COL_QKVG_C + 4 * W_RET
PROJ_TILE = 5 * MXU_DIM
PROJ_W = -(-(COL_AB + LANES) // PROJ_TILE) * PROJ_TILE

SPLIT_TILE = 2 * MXU_DIM
assert all(c % SPLIT_TILE == 0 for c in (COL_Z_A, COL_QKV_B, W_MOBA, COL_QKVG_C, COL_AB, PROJ_W))

GDN_CHUNK = 64
RET_CHUNK = 256
FAR_CHUNK = 4


def _t5_bucket_starts():
    max_exact = NUM_BUCKETS // 2
    d = np.arange(0, 4 * MAX_DISTANCE)
    dd = np.maximum(d, max_exact).astype(np.float64)
    large = max_exact + (np.log(dd / max_exact) / math.log(MAX_DISTANCE / max_exact)
                         * (NUM_BUCKETS - max_exact)).astype(np.int32)
    bucket = np.where(d < max_exact, d, np.minimum(large, NUM_BUCKETS - 1))
    return [int(np.argmax(bucket >= b)) for b in range(NUM_BUCKETS)]


_BUCKET_START = _t5_bucket_starts()


def _cparams(*sem):
    return pltpu.CompilerParams(dimension_semantics=sem, vmem_limit_bytes=VMEM_LIMIT)


def _silu(x):
    return x * (1.0 / (1.0 + jnp.exp(-x)))


def _dot(a, b):
    return jnp.dot(a.astype(bf16), b.astype(bf16), preferred_element_type=f32)


def _dot_nt(a, b):
    return lax.dot_general(a.astype(bf16), b.astype(bf16), (((1,), (1,)), ((), ())),
                           preferred_element_type=f32)


def _dot_tn(a, b):
    return lax.dot_general(a.astype(bf16), b.astype(bf16), (((0,), (0,)), ((), ())),
                           preferred_element_type=f32)


def _split(a):
    hi = a.astype(bf16)
    lo = (a - hi.astype(f32)).astype(bf16)
    return hi, lo


def _dot3(a, b):
    ah, al = _split(a)
    bh, bl = _split(b)
    d = lambda x, y: jnp.dot(x, y, preferred_element_type=f32)
    return d(ah, bh) + (d(ah, bl) + d(al, bh))


def _dot3_nt(a, b):
    ah, al = _split(a)
    bh, bl = _split(b)
    d = lambda x, y: lax.dot_general(x, y, (((1,), (1,)), ((), ())), preferred_element_type=f32)
    return d(ah, bh) + (d(ah, bl) + d(al, bh))


def _fold(x, op):
    return functools.reduce(op, [x[:, i * LANES:(i + 1) * LANES] for i in range(x.shape[1] // LANES)])


def _ada_kernel(c_ref, w_ref, b_ref, o_ref):
    o_ref[...] = _dot(_silu(c_ref[...]), w_ref[...]) + b_ref[...]


def ada_modulation(c_all, w_ada, b_ada, tn=1024):
    depth, d_model, n = w_ada.shape
    rows = c_all.shape[0]
    return pl.pallas_call(
        _ada_kernel,
        grid=(depth, n // tn),
        in_specs=[pl.BlockSpec((rows, d_model), lambda l, j: (0, 0)),
                  pl.BlockSpec((None, d_model, tn), lambda l, j: (l, 0, j)),
                  pl.BlockSpec((None, 1, tn), lambda l, j: (l, 0, j))],
        out_specs=pl.BlockSpec((None, rows, tn), lambda l, j: (l, 0, j)),
        out_shape=jax.ShapeDtypeStruct((depth, rows, n), f32),
        compiler_params=_cparams("arbitrary", "arbitrary"),
        name="ada_modulation",
    )(c_all, w_ada, b_ada.reshape(depth, 1, n))


def _norm_mod(x, gain, scale, shift):
    ms = jnp.mean(x * x, axis=-1, keepdims=True)
    return (x * lax.rsqrt(ms + RMS_EPS) * gain) * (1.0 + scale) + shift


def _nmm_kernel(x_ref, g_ref, sc_ref, sh_ref, w_ref, o_ref, h_ref):
    @pl.when(pl.program_id(1) == 0)
    def _():
        h_ref[...] = _norm_mod(x_ref[...], g_ref[...], sc_ref[...], sh_ref[...]).astype(bf16)

    o_ref[...] = jnp.dot(h_ref[...], w_ref[...], preferred_element_type=f32)


def _nmm_split_kernel(x_ref, g_ref, sc_ref, sh_ref, w_ref, *rest, tile_ranges):
    o_refs, h_ref = rest[:-1], rest[-1]
    j = pl.program_id(1)

    @pl.when(j == 0)
    def _():
        h_ref[...] = _norm_mod(x_ref[...], g_ref[...], sc_ref[...], sh_ref[...]).astype(bf16)

    y = jnp.dot(h_ref[...], w_ref[...], preferred_element_type=f32)
    for o_ref, (lo, hi) in zip(o_refs, tile_ranges):
        @pl.when((j >= lo) & (j < hi))
        def _(o_ref=o_ref):
            o_ref[...] = y


def norm_mod_matmul_split(x, gain, mod, sc_blk, sh_blk, w, layer, tm, tn, widths):
    m, d = x.shape
    assert mod.shape[0] == 1 and sum(widths) == w.shape[2] and all(wd % tn == 0 for wd in widths)
    bounds = np.cumsum([0] + [wd // tn for wd in widths])
    tile_ranges = [(int(lo), int(hi)) for lo, hi in zip(bounds[:-1], bounds[1:])]
    out_spec = lambda lo, hi: pl.BlockSpec((tm, tn), lambda i, j: (i, jnp.clip(j - lo, 0, hi - lo - 1)))
    return pl.pallas_call(
        functools.partial(_nmm_split_kernel, tile_ranges=tile_ranges),
        grid=(m // tm, w.shape[2] // tn),
        in_specs=[pl.BlockSpec((tm, d), lambda i, j: (i, 0)),
                  pl.BlockSpec((1, d), lambda i, j: (0, 0)),
                  pl.BlockSpec((1, d), lambda i, j: (0, sc_blk)),
                  pl.BlockSpec((1, d), lambda i, j: (0, sh_blk)),
                  pl.BlockSpec((None, d, tn), lambda i, j: (layer, 0, j))],
        out_specs=[out_spec(lo, hi) for lo, hi in tile_ranges],
        out_shape=[jax.ShapeDtypeStruct((m, wd), f32) for wd in widths],
        scratch_shapes=[pltpu.VMEM((tm, d), bf16)],
        compiler_params=_cparams("arbitrary", "arbitrary"),
        name="norm_mod_matmul_split",
    )(x, gain.reshape(1, d), mod, mod, w)


def norm_mod_matmul(x, gain, mod, sc_blk, sh_blk, w, layer, tm, tn):
    m, d = x.shape
    n = w.shape[2]
    per_row = mod.shape[0] != 1
    mrows = tm if per_row else 1
    mod_spec = lambda blk: pl.BlockSpec((mrows, d), (lambda i, j: (i, blk)) if per_row else (lambda i, j: (0, blk)))
    return pl.pallas_call(
        _nmm_kernel,
        grid=(m // tm, n // tn),
        in_specs=[pl.BlockSpec((tm, d), lambda i, j: (i, 0)),
                  pl.BlockSpec((1, d), lambda i, j: (0, 0)),
                  mod_spec(sc_blk), mod_spec(sh_blk),
                  pl.BlockSpec((None, d, tn), lambda i, j: (layer, 0, j))],
        out_specs=pl.BlockSpec((tm, tn), lambda i, j: (i, j)),
        out_shape=jax.ShapeDtypeStruct((m, n), f32),
        scratch_shapes=[pltpu.VMEM((tm, d), bf16)],
        compiler_params=_cparams("arbitrary", "arbitrary"),
        name="norm_mod_matmul",
    )(x, gain.reshape(1, d), mod, mod, w)


def _tri_inverse(lmat, blk):
    n = lmat.shape[0]
    row = lax.broadcasted_iota(jnp.int32, (n, n), 0)
    col = lax.broadcasted_iota(jnp.int32, (n, n), 1)
    eye = jnp.where(row == col, 1.0, 0.0).astype(f32)
    diag = jnp.where(row // SUBLANES == col // SUBLANES, lmat, 0.0)
    d2 = _dot3(diag, diag)
    d4 = _dot3(d2, d2)
    t = _dot3(_dot3(eye - diag, eye + d2), eye + d4)
    b = SUBLANES
    while b < blk:
        off = jnp.where((row // (2 * b) == col // (2 * b)) & (row // b != col // b), lmat, 0.0)
        t = t - _dot3(_dot3(t, off), t)
        b *= 2
    return t


def _cumsum_rows(x):
    row = lax.broadcasted_iota(jnp.int32, x.shape, 0)
    s = 1
    while s < x.shape[0]:
        x = x + jnp.where(row >= s, pltpu.roll(x, s, 0), 0.0)
        s *= 2
    return x


def _gdn_kernel(qkv_ref, z_ref, ab_ref, hist_ref, s0_ref, cw_ref, alog_ref, dtb_ref, nw_ref,
                o_ref, sfin_ref, s_scr, carry, *, chunk, n_valid):
    c_idx = pl.program_id(1)
    n_seq, rows, _ = qkv_ref.shape
    c = chunk
    n_sub = rows // c
    per_group = MXU_DIM // (H_GDN * c)
    units = [(s, u) for s in range(n_seq) for u in range(n_sub)]
    assert len(units) % per_group == 0

    @pl.when(c_idx == 0)
    def _():
        s_scr[...] = s0_ref[...]
        carry[...] = hist_ref[...]

    rowid = lax.broadcasted_iota(jnp.int32, (c, LANES), 0)
    valid = rowid < n_valid
    nw = nw_ref[...]

    prep = {}
    for s in range(n_seq):
        raw = qkv_ref[s]
        ext = jnp.concatenate([carry[s], raw], axis=0)
        carry[s] = raw[rows - SUBLANES:, :]
        conv = ext[SUBLANES:] * cw_ref[GDN_CONV - 1:GDN_CONV, :]
        for i in range(1, GDN_CONV):
            conv = conv + pltpu.roll(ext, i, 0)[SUBLANES:] * cw_ref[GDN_CONV - 1 - i:GDN_CONV - i, :]
        qkv_all = _silu(conv)
        for u in range(n_sub):
            qkv = qkv_all[u * c:(u + 1) * c]
            ab = ab_ref[s, u * c:(u + 1) * c, :]
            x = ab + dtb_ref[...]
            sp = jnp.maximum(x, 0.0) + jnp.log1p(jnp.exp(-jnp.abs(x)))
            g_all = jnp.where(valid, -jnp.exp(alog_ref[...]) * sp, 0.0)
            beta_all = jnp.where(valid, 1.0 / (1.0 + jnp.exp(-ab)), 0.0)
            gc_all = _cumsum_rows(g_all)
            heads = []
            for h in range(H_GDN):
                q = qkv[:, h * HEAD_DIM:(h + 1) * HEAD_DIM]
                k = qkv[:, W_GDN + h * HEAD_DIM:W_GDN + (h + 1) * HEAD_DIM]
                v = qkv[:, 2 * W_GDN + h * HEAD_DIM:2 * W_GDN + (h + 1) * HEAD_DIM]
                q = q * lax.rsqrt(jnp.sum(q * q, axis=-1, keepdims=True) + 1e-6) * HEAD_DIM ** -0.5
                k = k * lax.rsqrt(jnp.sum(k * k, axis=-1, keepdims=True) + 1e-6)
                beta = beta_all[:, H_GDN + h:H_GDN + h + 1]
                gc = gc_all[:, h:h + 1]
                g_last = gc_all[c - 1:c, h:h + 1]
                heads.append(dict(q=q, k=k, kb=k * beta, vb=v * beta, gc=gc, g_last=g_last))
            prep[(s, u)] = heads

    n = MXU_DIM
    row = lax.broadcasted_iota(jnp.int32, (n, n), 0)
    col = lax.broadcasted_iota(jnp.int32, (n, n), 1)
    same = row // c == col // c
    lower = same & (row >= col)
    strict = same & (row > col)

    groups = []
    for g in range(len(units) // per_group):
        pieces = [(su, h) for su in units[g * per_group:(g + 1) * per_group] for h in range(H_GDN)]
        cat = lambda key: jnp.concatenate([prep[su][h][key] for su, h in pieces], axis=0)
        k_all, kb_all, gcol = cat("k"), cat("kb"), cat("gc")
        grow = jnp.broadcast_to(gcol, (n, LANES)).T[0:1, :]
        decay = jnp.where(lower, jnp.exp(jnp.where(lower, gcol - grow, 0.0)), 0.0)
        lmat = jnp.where(strict, _dot_nt(kb_all, k_all) * decay, 0.0)
        tinv = _tri_inverse(lmat, c)
        rhs = jnp.concatenate([cat("vb"), kb_all * jnp.exp(gcol)], axis=1)
        uw = _dot(tinv, rhs)
        qk = _dot_nt(cat("q"), k_all) * decay
        groups.append((pieces, uw, qk))

    for pieces, uw, qk in groups:
        v_new = []
        for j, ((s, u), h) in enumerate(pieces):
            blk = uw[j * c:(j + 1) * c]
            v_new.append(blk[:, :HEAD_DIM] - _dot(blk[:, HEAD_DIM:], s_scr[s, h]))
        o_intra = _dot(qk, jnp.concatenate(v_new, axis=0))
        for j, ((s, u), h) in enumerate(pieces):
            p = prep[(s, u)][h]
            st = s_scr[s, h]
            o = _dot(p["q"] * jnp.exp(p["gc"]), st) + o_intra[j * c:(j + 1) * c]
            s_scr[s, h] = st * jnp.exp(p["g_last"]) + _dot_tn(p["k"] * jnp.exp(p["g_last"] - p["gc"]), v_new[j])
            o = o * lax.rsqrt(jnp.mean(o * o, axis=-1, keepdims=True) + RMS_EPS) * nw
            sl = slice(h * HEAD_DIM, (h + 1) * HEAD_DIM)
            o_ref[s, u * c:(u + 1) * c, sl] = o * _silu(z_ref[s, u * c:(u + 1) * c, sl])

    @pl.when(c_idx == pl.num_programs(1) - 1)
    def _():
        sfin_ref[...] = s_scr[...]


def gdn_mixer(qkv, z, ab, hist, s0, conv_w, a_log, dt_bias, norm_w, chunk, n_valid, n_seq, n_sub):
    s_n, t, _ = qkv[0].shape
    rows = n_sub * chunk
    assert s_n % n_seq == 0 and t % rows == 0 and (n_seq * n_sub * H_GDN * chunk) % MXU_DIM == 0
    lane_pad = lambda a: jnp.zeros((1, LANES), f32).at[0, :H_GDN].set(a)
    return pl.pallas_call(
        functools.partial(_gdn_kernel, chunk=chunk, n_valid=n_valid),
        grid=(s_n // n_seq, t // rows),
        in_specs=[pl.BlockSpec((n_seq, rows, 3 * W_GDN), lambda s, c: (s, c, qkv[1])),
                  pl.BlockSpec((n_seq, rows, W_GDN), lambda s, c: (s, c, z[1])),
                  pl.BlockSpec((n_seq, rows, LANES), lambda s, c: (s, c, ab[1])),
                  pl.BlockSpec((n_seq, SUBLANES, 3 * W_GDN), lambda s, c: (s, 0, 0)),
                  pl.BlockSpec((None, n_seq, H_GDN, HEAD_DIM, HEAD_DIM), lambda s, c: (s0[1], s, 0, 0, 0)),
                  pl.BlockSpec((GDN_CONV, 3 * W_GDN), lambda s, c: (0, 0)),
                  pl.BlockSpec((1, LANES), lambda s, c: (0, 0)),
                  pl.BlockSpec((1, LANES), lambda s, c: (0, 0)),
                  pl.BlockSpec((1, HEAD_DIM), lambda s, c: (0, 0))],
        out_specs=[pl.BlockSpec((n_seq, rows, W_GDN), lambda s, c: (s, c, 0)),
                   pl.BlockSpec((n_seq, H_GDN, HEAD_DIM, HEAD_DIM), lambda s, c: (s, 0, 0, 0))],
        out_shape=[jax.ShapeDtypeStruct((s_n, t, W_GDN), f32),
                   jax.ShapeDtypeStruct((s_n, H_GDN, HEAD_DIM, HEAD_DIM), f32)],
        scratch_shapes=[pltpu.VMEM((n_seq, H_GDN, HEAD_DIM, HEAD_DIM), f32),
                        pltpu.VMEM((n_seq, SUBLANES, 3 * W_GDN), f32)],
        compiler_params=_cparams("arbitrary", "arbitrary"),
        name="gdn_mixer",
    )(qkv[0], z[0], ab[0], hist, s0[0], conv_w, lane_pad(a_log), lane_pad(dt_bias), norm_w.reshape(1, HEAD_DIM))


def _ret_kernel(q_ref, k_ref, v_ref, g_ref, cs_ref, sn_ref, s0_ref, o_ref, sfin_ref, s_scr, *, n_valid):
    c_idx = pl.program_id(1)
    n_seq, c, _ = q_ref.shape
    c_len = min(c, n_valid)

    @pl.when(c_idx == 0)
    def _():
        s_scr[...] = s0_ref[...]

    cs = cs_ref[...]
    sn = sn_ref[...]
    rot = lambda x: x * cs + pltpu.roll(x, HEAD_DIM // 2, 1) * sn
    r2 = lax.broadcasted_iota(jnp.int32, (c, c), 0)
    c2 = lax.broadcasted_iota(jnp.int32, (c, c), 1)
    rel = (r2 - c2).astype(f32)
    idx = lax.broadcasted_iota(jnp.int32, (c, 1), 0)
    idx_f = idx.astype(f32)
    valid = idx < n_valid

    for h in range(H_RET):
        lg = math.log(1.0 - 2.0 ** (-5.0 - h))
        sl = slice(h * HEAD_DIM, (h + 1) * HEAD_DIM)
        dmat = jnp.where(rel >= 0, jnp.exp(jnp.maximum(rel, 0.0) * lg), 0.0)
        xi = jnp.exp((idx_f + 1.0) * lg)
        zeta = jnp.where(valid, jnp.exp((c_len - 1.0 - idx_f) * lg), 0.0)
        for s in range(n_seq):
            q = rot(q_ref[s, :, sl])
            k = rot(k_ref[s, :, sl]) * HEAD_DIM ** -0.5
            v = v_ref[s, :, sl]
            st = s_scr[s, h]
            o = _dot(_dot_nt(q, k) * dmat, v) + _dot(q * xi, st)
            s_scr[s, h] = math.exp(c_len * lg) * st + _dot_tn(k * zeta, v)
            o = o * lax.rsqrt(jnp.mean(o * o, axis=-1, keepdims=True) + RMS_EPS)
            o_ref[s, :, sl] = o * _silu(g_ref[s, :, sl])

    @pl.when(c_idx == pl.num_programs(1) - 1)
    def _():
        sfin_ref[...] = s_scr[...]


def retention_mixer(qkvg, cs, sn, s0, chunk, n_valid, n_seq):
    proj, base = qkvg
    s_n, t, _ = proj.shape
    assert s_n % n_seq == 0 and t % chunk == 0
    col_spec = lambda part: pl.BlockSpec((n_seq, chunk, W_RET), lambda s, c: (s, c, base + part))
    return pl.pallas_call(
        functools.partial(_ret_kernel, n_valid=n_valid),
        grid=(s_n // n_seq, t // chunk),
        in_specs=[col_spec(0), col_spec(1), col_spec(2), col_spec(3),
                  pl.BlockSpec((chunk, HEAD_DIM), lambda s, c: (c, 0)),
                  pl.BlockSpec((chunk, HEAD_DIM), lambda s, c: (c, 0)),
                  pl.BlockSpec((None, n_seq, H_RET, HEAD_DIM, HEAD_DIM), lambda s, c: (s0[1], s, 0, 0, 0))],
        out_specs=[pl.BlockSpec((n_seq, chunk, W_RET), lambda s, c: (s, c, 0)),
                   pl.BlockSpec((n_seq, H_RET, HEAD_DIM, HEAD_DIM), lambda s, c: (s, 0, 0, 0))],
        out_shape=[jax.ShapeDtypeStruct((s_n, t, W_RET), f32),
                   jax.ShapeDtypeStruct((s_n, H_RET, HEAD_DIM, HEAD_DIM), f32)],
        scratch_shapes=[pltpu.VMEM((n_seq, H_RET, HEAD_DIM, HEAD_DIM), f32)],
        compiler_params=_cparams("arbitrary", "arbitrary"),
        name="retention_mixer",
    )(proj, proj, proj, proj, cs, sn, s0[0])


def _rotary_tables(pos):
    half = HEAD_DIM // 2
    inv = 1.0 / (10000.0 ** jnp.linspace(0.0, 1.0, half, dtype=f32))
    ang = pos.astype(f32)[:, None] * inv[None, :]
    cos, sin = jnp.cos(ang), jnp.sin(ang)
    return jnp.concatenate([cos, cos], axis=1), jnp.concatenate([-sin, sin], axis=1)


def _t5_bias_shifted(dist, tab):
    val = tab(0)
    for b in range(1, NUM_BUCKETS):
        val = jnp.where(dist >= _BUCKET_START[b], tab(b), val)
    return val - tab(NUM_BUCKETS - 1)


def _top_blocks(gate, n_sel):
    lane = lax.broadcasted_iota(jnp.int32, gate.shape, 1)
    sel = jnp.zeros(gate.shape, f32)
    for _ in range(n_sel):
        m = jnp.max(gate, axis=1, keepdims=True)
        first = jnp.min(jnp.where(gate == m, lane, gate.shape[1]), axis=1, keepdims=True)
        pick = (lane == first) & (m > 0.5 * NEG_INF)
        sel = jnp.where(pick, 1.0, sel)
        gate = jnp.where(pick, NEG_INF, gate)
    return sel


def _scores(q_aug, k_aug):
    return lax.dot_general(q_aug, k_aug, (((1,), (1,)), ((), ())), preferred_element_type=f32)


def _moba_prompt_kernel(tab_ref, q_ref, k_ref, v_ref, o_ref,
                        kaug_scr, v_scr, kmean_scr, bias_scr, s_scr, l_scr, acc_scr):
    h = pl.program_id(0)
    i = pl.program_id(1)
    blk = MOBA_BLOCK
    t = k_ref.shape[0]
    n_blocks = t // blk
    ck = FAR_CHUNK * blk
    masked_lane = LANES - 1

    @pl.when(i == 0)
    def _():
        k = k_ref[...]
        kmean_scr[...] = jnp.zeros(kmean_scr.shape, f32)
        kmean_scr[0:n_blocks, :] = jnp.mean(k.reshape(n_blocks, blk, HEAD_DIM), axis=1)
        row = lax.broadcasted_iota(jnp.int32, (t, LANES), 0)
        lane_t = lax.broadcasted_iota(jnp.int32, (t, LANES), 1)
        kaug_scr[blk:blk + t, 0:HEAD_DIM] = k.astype(bf16)
        kaug_scr[blk:blk + t, HEAD_DIM:] = jnp.where(lane_t == row // blk, 1.0, 0.0).astype(bf16)
        v_scr[blk:blk + t, :] = v_ref[...].astype(bf16)
        for start, size in ((0, blk), (blk + t, kaug_scr.shape[0] - blk - t)):
            if not size:
                continue
            lane_p = lax.broadcasted_iota(jnp.int32, (size, LANES), 1)
            kaug_scr[start:start + size, 0:HEAD_DIM] = jnp.zeros((size, HEAD_DIM), bf16)
            kaug_scr[start:start + size, HEAD_DIM:] = jnp.where(lane_p == masked_lane, 1.0, 0.0).astype(bf16)
            v_scr[start:start + size, :] = jnp.zeros((size, HEAD_DIM), bf16)
        r2 = lax.broadcasted_iota(jnp.int32, (blk, blk), 0)
        c2 = lax.broadcasted_iota(jnp.int32, (blk, blk), 1)
        tab = lambda b: tab_ref[b, h]
        bias_scr[0] = _t5_bias_shifted(r2 - c2 + blk, tab)
        bias_scr[1] = jnp.where(r2 >= c2, _t5_bias_shifted(r2 - c2, tab), NEG_INF)

    q = q_ref[...]
    lane = lax.broadcasted_iota(jnp.int32, (blk, LANES), 1)
    gate = jnp.where(lane < i, _dot3_nt(q, kmean_scr[...]), NEG_INF)
    sel = _top_blocks(gate, MOBA_TOPK) > 0.5
    qs = q * HEAD_DIM ** -0.5
    neg_far = jnp.where(sel & (lane < i - 1), 0.0, NEG_INF)
    neg_near = jnp.where((sel & (lane == i - 1)) | (lane == i), 0.0, NEG_INF)
    q_far = jnp.concatenate([qs, neg_far], axis=1).astype(bf16)
    q_near = jnp.concatenate([qs, neg_near], axis=1).astype(bf16)
    n_chunks = (jnp.maximum(i - 1, 0) + FAR_CHUNK - 1) // FAR_CHUNK

    far_rows = lambda c: pl.ds(pl.multiple_of(blk + c * ck, blk), ck)

    def scores_pass(c, mx):
        s = _scores(q_far, kaug_scr[far_rows(c), :])
        s_scr[c] = s
        return jnp.maximum(mx, _fold(s, jnp.maximum))

    mx = lax.fori_loop(0, n_chunks, scores_pass, jnp.full((blk, LANES), NEG_INF, f32))
    near_rows = pl.ds(pl.multiple_of(i * blk, blk), 2 * blk)
    s_near = _scores(q_near, kaug_scr[near_rows, :]) + jnp.concatenate([bias_scr[0], bias_scr[1]], axis=1)
    m = jnp.max(jnp.maximum(mx, _fold(s_near, jnp.maximum)), axis=1, keepdims=True)
    p_near = jnp.exp(s_near - m)
    l_scr[...] = _fold(p_near, jnp.add)
    acc_scr[...] = jnp.dot(p_near.astype(bf16), v_scr[near_rows, :], preferred_element_type=f32)

    def values_pass(c, carry):
        p = jnp.exp(s_scr[c] - m)
        l_scr[...] += _fold(p, jnp.add)
        acc_scr[...] += jnp.dot(p.astype(bf16), v_scr[far_rows(c), :], preferred_element_type=f32)
        return carry

    lax.fori_loop(0, n_chunks, values_pass, 0)
    o_ref[...] = acc_scr[...] / jnp.sum(l_scr[...], axis=1, keepdims=True)


def moba_prompt(q, k, v, rel_bias):
    t = q[0].shape[0]
    blk = MOBA_BLOCK
    n_blocks = t // blk
    assert t % blk == 0 and n_blocks < LANES - 1 and blk >= _BUCKET_START[-1]
    qb, kb, vb = q[1], k[1], v[1]
    far_max = -(-max(n_blocks - 2, 1) // FAR_CHUNK) * FAR_CHUNK
    kv_rows = (1 + max(n_blocks, far_max + 1)) * blk
    return pl.pallas_call(
        _moba_prompt_kernel,
        grid=(H_MOBA, n_blocks),
        in_specs=[pl.BlockSpec(memory_space=pltpu.SMEM),
                  pl.BlockSpec((blk, HEAD_DIM), lambda h, i: (i, qb + h)),
                  pl.BlockSpec((t, HEAD_DIM), lambda h, i: (0, kb + h)),
                  pl.BlockSpec((t, HEAD_DIM), lambda h, i: (0, vb + h))],
        out_specs=pl.BlockSpec((blk, HEAD_DIM), lambda h, i: (i, h)),
        out_shape=jax.ShapeDtypeStruct((t, W_MOBA), f32),
        scratch_shapes=[pltpu.VMEM((kv_rows, 2 * HEAD_DIM), bf16),
                        pltpu.VMEM((kv_rows, HEAD_DIM), bf16),
                        pltpu.VMEM((LANES, HEAD_DIM), f32),
                        pltpu.VMEM((2, blk, blk), f32),
                        pltpu.VMEM((far_max // FAR_CHUNK, blk, FAR_CHUNK * blk), f32),
                        pltpu.VMEM((blk, LANES), f32),
                        pltpu.VMEM((blk, HEAD_DIM), f32)],
        compiler_params=_cparams("arbitrary", "arbitrary"),
        name="moba_prompt",
    )(rel_bias, q[0], k[0], v[0])


def _moba_sample_kernel(pt_ref, q_ref, kn_ref, vn_ref, tabr_ref, *rest, n_pages, n_new):
    kp = rest[:n_pages]
    vp = rest[n_pages:2 * n_pages]
    o_ref, onehot_scr, near_scr, own_scr, s_scr = rest[2 * n_pages:]
    page = kp[0].shape[0]
    pr = page * H_MOBA
    past = n_pages * page
    ppb = MOBA_BLOCK // page
    n_blocks = n_pages // ppb
    rows = H_MOBA * n_new
    n_own = kn_ref.shape[0]
    near = [p for p in range(n_pages) if past - (p * page + page - 1) < _BUCKET_START[-1]]

    r_l = lax.broadcasted_iota(jnp.int32, (rows, LANES), 0)
    l_l = lax.broadcasted_iota(jnp.int32, (rows, LANES), 1)
    head_r = r_l // n_new
    t_r = r_l % n_new

    @pl.when(pl.program_id(0) == 0)
    def _():
        kr = lax.broadcasted_iota(jnp.int32, (pr, LANES), 0)
        kl = lax.broadcasted_iota(jnp.int32, (pr, LANES), 1)
        for n in range(n_blocks):
            onehot_scr[n] = jnp.where(kl == n * H_MOBA + kr % H_MOBA, 1.0, 0.0).astype(bf16)
        tabr = tabr_ref[...]
        tab = lambda b: tabr[:, b:b + 1]
        rr = lax.broadcasted_iota(jnp.int32, (rows, pr), 0)
        cc = lax.broadcasted_iota(jnp.int32, (rows, pr), 1)
        for idx, p in enumerate(near):
            near_scr[idx] = _t5_bias_shifted(past + rr % n_new - (p * page + cc // H_MOBA), tab)
        t_k = l_l // H_MOBA
        visible = (l_l % H_MOBA == head_r) & (t_k <= t_r) & (t_k < n_new)
        own_scr[...] = jnp.where(visible, _t5_bias_shifted(t_r - t_k, tab), NEG_INF)

    q = q_ref[...]
    kmean = []
    for n in range(n_blocks):
        tot = functools.reduce(jnp.add, [jnp.sum(kp[p][...], axis=0) for p in range(n * ppb, (n + 1) * ppb)])
        kmean.append(tot * (1.0 / MOBA_BLOCK))
    kmean.append(jnp.zeros((LANES - n_blocks * H_MOBA, HEAD_DIM), f32))
    gate = _dot3_nt(q, jnp.concatenate(kmean, axis=0))
    own_head = (l_l < n_blocks * H_MOBA) & (l_l % H_MOBA == head_r)
    sel = _top_blocks(jnp.where(own_head, gate, NEG_INF), min(MOBA_TOPK, n_blocks))
    qs = q * HEAD_DIM ** -0.5
    q_aug = jnp.concatenate([qs, jnp.where(sel > 0.5, 0.0, NEG_INF)], axis=1).astype(bf16)

    mx = jnp.full((rows, LANES), NEG_INF, f32)
    for p in range(n_pages):
        k_aug = jnp.concatenate([kp[p][...].reshape(pr, HEAD_DIM).astype(bf16), onehot_scr[p // ppb]], axis=1)
        s = _scores(q_aug, k_aug)
        if p in near:
            s = s + near_scr[near.index(p)]
        s_scr[:, p * pr:(p + 1) * pr] = s
        mx = jnp.maximum(mx, _fold(s, jnp.maximum))
    pad_own = lambda r: jnp.concatenate([r[...], jnp.zeros((LANES - n_own, HEAD_DIM), f32)], axis=0)
    s_own = _dot_nt(qs, pad_own(kn_ref)) + own_scr[...]
    m = jnp.max(jnp.maximum(mx, s_own), axis=1, keepdims=True)
    p_own = jnp.exp(s_own - m)
    lsum = p_own
    acc = _dot(p_own, pad_own(vn_ref))
    for p in range(n_pages):
        pe = jnp.exp(s_scr[:, p * pr:(p + 1) * pr] - m)
        lsum = lsum + _fold(pe, jnp.add)
        acc = acc + _dot(pe, vp[p][...].reshape(pr, HEAD_DIM))
    o_ref[...] = acc / jnp.sum(lsum, axis=1, keepdims=True)


def moba_sample(q, k_new, v_new, cache_k, cache_v, page_table, rel_bias, layer, n_new):
    bsz, rows, _ = q.shape
    n_pages = page_table.shape[1]
    page = cache_k.shape[2]
    past = n_pages * page
    n_blocks = past // MOBA_BLOCK
    assert past % MOBA_BLOCK == 0 and MOBA_BLOCK % page == 0 and n_blocks * H_MOBA <= LANES
    assert rows == H_MOBA * n_new and k_new.shape[1] <= LANES
    n_near = len([p for p in range(n_pages) if past - (p * page + page - 1) < _BUCKET_START[-1]])
    tab_rows = jnp.pad(jnp.repeat(rel_bias.T, n_new, axis=0), ((0, 0), (0, LANES - NUM_BUCKETS)))
    page_spec = lambda p: pl.BlockSpec((None, None, page, H_MOBA, HEAD_DIM),
                                       lambda b, pt: (layer, pt[b, p], 0, 0, 0))
    row_spec = lambda a: pl.BlockSpec((None,) + a.shape[1:], lambda b, pt: (b, 0, 0))
    grid_spec = pltpu.PrefetchScalarGridSpec(
        num_scalar_prefetch=1,
        grid=(bsz,),
        in_specs=([row_spec(q), row_spec(k_new), row_spec(v_new),
                   pl.BlockSpec(tab_rows.shape, lambda b, pt: (0, 0))]
                  + [page_spec(p) for p in range(n_pages)] * 2),
        out_specs=row_spec(q),
        scratch_shapes=[pltpu.VMEM((n_blocks, page * H_MOBA, LANES), bf16),
                        pltpu.VMEM((max(n_near, 1), rows, page * H_MOBA), f32),
                        pltpu.VMEM((rows, LANES), f32),
                        pltpu.VMEM((rows, past * H_MOBA), f32)],
    )
    return pl.pallas_call(
        functools.partial(_moba_sample_kernel, n_pages=n_pages, n_new=n_new),
        grid_spec=grid_spec,
        out_shape=jax.ShapeDtypeStruct(q.shape, f32),
        compiler_params=_cparams("arbitrary"),
        name="moba_sample",
    )(page_table, q, k_new, v_new, tab_rows, *([cache_k] * n_pages), *([cache_v] * n_pages))


def _post(x, y, gain, gate):
    return x + gate * (y * lax.rsqrt(jnp.mean(y * y, axis=-1, keepdims=True) + RMS_EPS) * gain)


def _out_proj_kernel(oa_ref, ob_ref, oc_ref, w_ref, x_ref, gt_ref, g_ref, o_ref):
    y = (jnp.dot(oa_ref[...].astype(bf16), w_ref[0:W_GDN, :], preferred_element_type=f32)
         + jnp.dot(ob_ref[...].astype(bf16), w_ref[W_GDN:W_GDN + W_MOBA, :], preferred_element_type=f32)
         + jnp.dot(oc_ref[...].astype(bf16), w_ref[W_GDN + W_MOBA:, :], preferred_element_type=f32))
    o_ref[...] = _post(x_ref[...], y, g_ref[...], gt_ref[...])


def out_proj_residual(o_a, o_b, o_c, w_out, layer, x, mod, gt_blk, gain, tm):
    m, d = x.shape
    per_row = mod.shape[0] != 1
    mod_spec = pl.BlockSpec((tm if per_row else 1, d), (lambda i: (i, gt_blk)) if per_row else (lambda i: (0, gt_blk)))
    return pl.pallas_call(
        _out_proj_kernel,
        grid=(m // tm,),
        in_specs=[pl.BlockSpec((tm, W_GDN), lambda i: (i, 0)),
                  pl.BlockSpec((tm, W_MOBA), lambda i: (i, 0)),
                  pl.BlockSpec((tm, W_RET), lambda i: (i, 0)),
                  pl.BlockSpec((None,) + w_out.shape[1:], lambda i: (layer, 0, 0)),
                  pl.BlockSpec((tm, d), lambda i: (i, 0)),
                  mod_spec,
                  pl.BlockSpec((1, d), lambda i: (0, 0))],
        out_specs=pl.BlockSpec((tm, d), lambda i: (i, 0)),
        out_shape=jax.ShapeDtypeStruct((m, d), f32),
        compiler_params=_cparams("arbitrary"),
        name="out_proj_residual",
    )(o_a, o_b, o_c, w_out, x, mod, gain.reshape(1, d))


def _ffn_down_kernel(ug_ref, uu_ref, pg_ref, pu_ref, hg_ref, hu_ref, cwg_ref, cwu_ref, wd_ref,
                     x_ref, gt_ref, g_ref, o_ref, acc_ref, *, row_shift):
    i = pl.program_id(0)
    f = pl.program_id(1)
    hdr = hg_ref.shape[0]
    first = i == 0

    def conv(u_ref, p_ref, h_ref, cw_ref):
        head = jnp.where(first, h_ref[...], p_ref[...])
        ext = jnp.concatenate([head, u_ref[...]], axis=0)
        out = ext[hdr:] * cw_ref[FFN_CONV - 1:FFN_CONV, :]
        for j in range(1, FFN_CONV):
            out = out + pltpu.roll(ext, j * row_shift, 0)[hdr:] * cw_ref[FFN_CONV - 1 - j:FFN_CONV - j, :]
        return out

    act = _silu(conv(ug_ref, pg_ref, hg_ref, cwg_ref)) * conv(uu_ref, pu_ref, hu_ref, cwu_ref)
    part = jnp.dot(act.astype(bf16), wd_ref[...], preferred_element_type=f32)

    @pl.when(f == 0)
    def _():
        acc_ref[...] = part

    @pl.when(f > 0)
    def _():
        acc_ref[...] += part

    @pl.when(f == pl.num_programs(1) - 1)
    def _():
        o_ref[...] = _post(x_ref[...], acc_ref[...], g_ref[...], gt_ref[...])


def ffn_down_residual(u, hist, conv_w, w_down, layer, x, mod, gt_blk, gain, tm, tf, row_shift):
    m, d = x.shape
    ff = w_down.shape[1]
    nf = ff // tf
    hdr = hist.shape[0]
    assert hdr >= (FFN_CONV - 1) * row_shift and tm % hdr == 0
    per_row = mod.shape[0] != 1
    mod_spec = pl.BlockSpec((tm if per_row else 1, d),
                            (lambda i, f: (i, gt_blk)) if per_row else (lambda i, f: (0, gt_blk)))
    prev = lambda off: (lambda i, f: (jnp.maximum(i * (tm // hdr) - 1, 0), f + off))
    return pl.pallas_call(
        functools.partial(_ffn_down_kernel, row_shift=row_shift),
        grid=(m // tm, nf),
        in_specs=[pl.BlockSpec((tm, tf), lambda i, f: (i, f)),
                  pl.BlockSpec((tm, tf), lambda i, f: (i, f + nf)),
                  pl.BlockSpec((hdr, tf), prev(0)),
                  pl.BlockSpec((hdr, tf), prev(nf)),
                  pl.BlockSpec((hdr, tf), lambda i, f: (0, f)),
                  pl.BlockSpec((hdr, tf), lambda i, f: (0, f + nf)),
                  pl.BlockSpec((FFN_CONV, tf), lambda i, f: (0, f)),
                  pl.BlockSpec((FFN_CONV, tf), lambda i, f: (0, f + nf)),
                  pl.BlockSpec((None, tf, d), lambda i, f: (layer, f, 0)),
                  pl.BlockSpec((tm, d), lambda i, f: (i, 0)),
                  mod_spec,
                  pl.BlockSpec((1, d), lambda i, f: (0, 0))],
        out_specs=pl.BlockSpec((tm, d), lambda i, f: (i, 0)),
        out_shape=jax.ShapeDtypeStruct((m, d), f32),
        scratch_shapes=[pltpu.VMEM((tm, d), f32)],
        compiler_params=_cparams("arbitrary", "arbitrary"),
        name="ffn_down_residual",
    )(u, u, u, u, hist, hist, conv_w, conv_w, w_down, x, mod, gain.reshape(1, d))


def _permute_w_in(w_in):
    s1 = 3 * W_GDN
    s2 = s1 + W_GDN
    s4 = s2 + 2 * H_GDN
    ab = w_in[..., s2:s4]
    pad = jnp.zeros(w_in.shape[:-1] + (PROJ_W - COL_AB - 2 * H_GDN,), w_in.dtype)
    return jnp.concatenate([w_in[..., :s2], w_in[..., s4:], ab, pad], axis=-1).astype(bf16)


def kernel(x_prompt, x_sample, c_prompt, c_sample, cache_k, cache_v, page_table, state_gdn_conv, state_gdn, state_ret, state_ffn_conv, w_ada, b_ada, g_pre_mix, g_post_mix, g_pre_ffn, g_post_ffn, w_in, gdn_conv_w, gdn_a_log, gdn_dt_bias, gdn_norm_w, rel_bias, w_out, w_up, ffn_conv_w, w_down):
    bp, seq, d = x_prompt.shape
    bs, dec = x_sample.shape[:2]
    depth = w_in.shape[0]
    assert bp == 1 and dec <= SUBLANES
    past_len = page_table.shape[1] * cache_k.shape[2]
    d_ff = w_down.shape[1]
    pad_t = SUBLANES - dec
    seq_per_step = MXU_DIM // (H_GDN * SUBLANES)

    w_in_b = _permute_w_in(w_in)
    w_out_b = w_out.astype(bf16)
    w_up_b = w_up.astype(bf16)
    w_down_b = w_down.astype(bf16)

    n_c = bp + bs
    c_all = jnp.concatenate([c_prompt, c_sample, jnp.zeros((-n_c % SUBLANES, d), f32)], axis=0)
    mod = ada_modulation(c_all, w_ada, b_ada)

    cs_p, sn_p = _rotary_tables(jnp.arange(seq, dtype=jnp.int32))
    cs_s, sn_s = _rotary_tables(past_len + jnp.arange(SUBLANES, dtype=jnp.int32))

    yp = x_prompt.reshape(seq, d)
    ys = jnp.swapaxes(x_sample, 0, 1).reshape(dec * bs, d)
    to_bm = lambda a: jnp.pad(jnp.swapaxes(a.reshape(dec, bs, -1), 0, 1), ((0, 0), (0, pad_t), (0, 0)))
    to_tm = lambda a: jnp.swapaxes(a[:, :dec], 0, 1).reshape(dec * bs, -1)
    heads_tm = lambda a: a.reshape(dec, bs, H_MOBA, HEAD_DIM)

    assert H_GDN == H_RET
    zero_state = jnp.zeros((1, bp, H_GDN, HEAD_DIM, HEAD_DIM), f32)

    outs = [[] for _ in range(12)]
    for l in range(depth):
        mod_p = mod[l, 0:bp]
        mod_s = jnp.tile(mod[l, bp:bp + bs], (dec, 1))

        qkva_p, z_p, qb_p, kb_p, vb_p, c_p, ab_p = norm_mod_matmul_split(
            yp, g_pre_mix[l], mod_p, 1, 0, w_in_b, l, tm=512, tn=SPLIT_TILE,
            widths=(3 * W_GDN, W_GDN, W_MOBA, W_MOBA, W_MOBA, 4 * W_RET, PROJ_W - COL_AB))
        oa_p, gs_p = gdn_mixer((qkva_p[None], 0), (z_p[None], 0), (ab_p[None], 0),
                               jnp.zeros((1, SUBLANES, 3 * W_GDN), f32), (zero_state, 0),
                               gdn_conv_w[l], gdn_a_log[l], gdn_dt_bias[l], gdn_norm_w[l],
                               chunk=GDN_CHUNK, n_valid=GDN_CHUNK, n_seq=1, n_sub=2)
        oc_p, rs_p = retention_mixer((c_p[None], 0), cs_p, sn_p, (zero_state, 0),
                                     chunk=RET_CHUNK, n_valid=RET_CHUNK, n_seq=1)
        ob_p = moba_prompt((qb_p, 0), (kb_p, 0), (vb_p, 0), rel_bias)
        yp = out_proj_residual(oa_p[0], ob_p, oc_p[0], w_out_b, l, yp, mod_p, 2, g_post_mix[l], tm=512)

        proj_s = norm_mod_matmul(ys, g_pre_mix[l], mod_s, 1, 0, w_in_b, l, tm=dec * bs, tn=PROJ_TILE)
        proj_sb = to_bm(proj_s)
        hist_s = jnp.pad(state_gdn_conv[l], ((0, 0), (SUBLANES - (GDN_CONV - 1), 0), (0, 0)))
        oa_s, gs_s = gdn_mixer((proj_sb, COL_QKV_A // (3 * W_GDN)), (proj_sb, COL_Z_A // W_GDN),
                               (proj_sb, COL_AB // LANES), hist_s, (state_gdn, l), gdn_conv_w[l], gdn_a_log[l],
                               gdn_dt_bias[l], gdn_norm_w[l], chunk=SUBLANES, n_valid=dec, n_seq=seq_per_step,
                               n_sub=1)
        oc_s, rs_s = retention_mixer((proj_sb, COL_QKVG_C // W_RET), cs_s, sn_s, (state_ret, l), chunk=SUBLANES,
                                     n_valid=dec, n_seq=seq_per_step)
        q_s, k_s, v_s = [heads_tm(proj_s[:, COL_QKV_B + j * W_MOBA:COL_QKV_B + (j + 1) * W_MOBA]) for j in range(3)]
        k_s, v_s = jnp.swapaxes(k_s, 0, 1), jnp.swapaxes(v_s, 0, 1)
        ob_s = moba_sample(jnp.transpose(q_s, (1, 2, 0, 3)).reshape(bs, H_MOBA * dec, HEAD_DIM),
                           k_s.reshape(bs, dec * H_MOBA, HEAD_DIM), v_s.reshape(bs, dec * H_MOBA, HEAD_DIM),
                           cache_k, cache_v, page_table, rel_bias, l, dec)
        ob_s = jnp.transpose(ob_s.reshape(bs, H_MOBA, dec, HEAD_DIM), (2, 0, 1, 3)).reshape(dec * bs, W_MOBA)
        ys = out_proj_residual(to_tm(oa_s), ob_s, to_tm(oc_s), w_out_b, l, ys, mod_s, 2,
                               g_post_mix[l], tm=dec * bs)

        u_p = norm_mod_matmul(yp, g_pre_ffn[l], mod_p, 4, 3, w_up_b, l, tm=1024, tn=1024)
        yp = ffn_down_residual(u_p, jnp.zeros((SUBLANES, 2 * d_ff), f32), ffn_conv_w[l], w_down_b, l, yp,
                               mod_p, 5, g_post_ffn[l], tm=512, tf=d_ff // 4, row_shift=1)
        u_s = norm_mod_matmul(ys, g_pre_ffn[l], mod_s, 4, 3, w_up_b, l, tm=dec * bs, tn=1024)
        hist_f = jnp.swapaxes(state_ffn_conv[l], 0, 1).reshape((FFN_CONV - 1) * bs, 2 * d_ff)
        ys = ffn_down_residual(u_s, hist_f, ffn_conv_w[l], w_down_b, l, ys, mod_s, 5, g_post_ffn[l],
                               tm=dec * bs, tf=512, row_shift=bs)

        outs[0].append(kb_p.reshape(bp, seq, H_MOBA, HEAD_DIM))
        outs[1].append(vb_p.reshape(bp, seq, H_MOBA, HEAD_DIM))
        outs[2].append(k_s)
        outs[3].append(v_s)
        outs[4].append(qkva_p[seq - (GDN_CONV - 1):].reshape(bp, GDN_CONV - 1, 3 * W_GDN))
        outs[5].append(proj_sb[:, dec - (GDN_CONV - 1):dec, :3 * W_GDN])
        outs[6].append(gs_p)
        outs[7].append(gs_s)
        outs[8].append(rs_p)
        outs[9].append(rs_s)
        outs[10].append(u_p[seq - (FFN_CONV - 1):].reshape(bp, FFN_CONV - 1, 2 * d_ff))
        outs[11].append(jnp.swapaxes(u_s.reshape(dec, bs, 2 * d_ff)[dec - (FFN_CONV - 1):], 0, 1))

    y_prompt = yp.reshape(bp, seq, d)
    y_sample = jnp.swapaxes(ys.reshape(dec, bs, d), 0, 1)
    return (y_prompt, y_sample) + tuple(jnp.stack(o, axis=0) for o in outs)
```

```python
import functools
import math

import numpy as np
import jax
import jax.numpy as jnp
from jax import lax
from jax.experimental import pallas as pl
from jax.experimental.pallas import tpu as pltpu

f32 = jnp.float32
bf16 = jnp.bfloat16

HEAD_DIM = 128
H_GDN = 4
H_MOBA = 8
H_RET = 4
W_GDN = H_GDN * HEAD_DIM
W_MOBA = H_MOBA * HEAD_DIM
W_RET = H_RET * HEAD_DIM
GDN_CONV = 4
FFN_CONV = 3
MOBA_BLOCK = 256
MOBA_TOPK = 3
NUM_BUCKETS = 32
MAX_DISTANCE = 128
RMS_EPS = 1e-6
NEG_INF = -1e30

SUBLANES = 8
LANES = 128
MXU_DIM = 256
VMEM_LIMIT = 56 * 1024 * 1024

COL_QKV_A = 0
COL_Z_A = 3 * W_GDN
COL_QKV_B = COL_Z_A + W_GDN
COL_QKVG_C = COL_QKV_B + 3 * W_MOBA
COL_AB = COL_QKVG_C + 4 * W_RET
PROJ_TILE = 5 * MXU_DIM
PROJ_W = -(-(COL_AB + LANES) // PROJ_TILE) * PROJ_TILE

SPLIT_TILE = 2 * MXU_DIM
assert all(c % SPLIT_TILE == 0 for c in (COL_Z_A, COL_QKV_B, W_MOBA, COL_QKVG_C, COL_AB, PROJ_W))

GDN_CHUNK = 64
RET_CHUNK = 256
FAR_CHUNK = 4


def _t5_bucket_starts():
    max_exact = NUM_BUCKETS // 2
    d = np.arange(0, 4 * MAX_DISTANCE)
    dd = np.maximum(d, max_exact).astype(np.float64)
    large = max_exact + (np.log(dd / max_exact) / math.log(MAX_DISTANCE / max_exact)
                         * (NUM_BUCKETS - max_exact)).astype(np.int32)
    bucket = np.where(d < max_exact, d, np.minimum(large, NUM_BUCKETS - 1))
    return [int(np.argmax(bucket >= b)) for b in range(NUM_BUCKETS)]


_BUCKET_START = _t5_bucket_starts()


def _cparams(*sem):
    return pltpu.CompilerParams(dimension_semantics=sem, vmem_limit_bytes=VMEM_LIMIT)


def _silu(x):
    return x * (1.0 / (1.0 + jnp.exp(-x)))


def _dot(a, b):
    return jnp.dot(a.astype(bf16), b.astype(bf16), preferred_element_type=f32)


def _dot_nt(a, b):
    return lax.dot_general(a.astype(bf16), b.astype(bf16), (((1,), (1,)), ((), ())),
                           preferred_element_type=f32)


def _dot_tn(a, b):
    return lax.dot_general(a.astype(bf16), b.astype(bf16), (((0,), (0,)), ((), ())),
                           preferred_element_type=f32)


def _split(a):
    hi = a.astype(bf16)
    lo = (a - hi.astype(f32)).astype(bf16)
    return hi, lo


def _dot3(a, b):
    ah, al = _split(a)
    bh, bl = _split(b)
    d = lambda x, y: jnp.dot(x, y, preferred_element_type=f32)
    return d(ah, bh) + (d(ah, bl) + d(al, bh))


def _dot3_nt(a, b):
    ah, al = _split(a)
    bh, bl = _split(b)
    d = lambda x, y: lax.dot_general(x, y, (((1,), (1,)), ((), ())), preferred_element_type=f32)
    return d(ah, bh) + (d(ah, bl) + d(al, bh))


def _fold(x, op):
    return functools.reduce(op, [x[:, i * LANES:(i + 1) * LANES] for i in range(x.shape[1] // LANES)])


def _ada_kernel(c_ref, w_ref, b_ref, o_ref):
    o_ref[...] = _dot(_silu(c_ref[...]), w_ref[...]) + b_ref[...]


def ada_modulation(c_all, w_ada, b_ada, tn=1024):
    depth, d_model, n = w_ada.shape
    rows = c_all.shape[0]
    return pl.pallas_call(
        _ada_kernel,
        grid=(depth, n // tn),
        in_specs=[pl.BlockSpec((rows, d_model), lambda l, j: (0, 0)),
                  pl.BlockSpec((None, d_model, tn), lambda l, j: (l, 0, j)),
                  pl.BlockSpec((None, 1, tn), lambda l, j: (l, 0, j))],
        out_specs=pl.BlockSpec((None, rows, tn), lambda l, j: (l, 0, j)),
        out_shape=jax.ShapeDtypeStruct((depth, rows, n), f32),
        compiler_params=_cparams("arbitrary", "arbitrary"),
        name="ada_modulation",
    )(c_all, w_ada, b_ada.reshape(depth, 1, n))


def _norm_mod(x, gain, scale, shift):
    ms = jnp.mean(x * x, axis=-1, keepdims=True)
    return (x * lax.rsqrt(ms + RMS_EPS) * gain) * (1.0 + scale) + shift


def _nmm_kernel(x_ref, g_ref, sc_ref, sh_ref, w_ref, o_ref, h_ref):
    @pl.when(pl.program_id(1) == 0)
    def _():
        h_ref[...] = _norm_mod(x_ref[...], g_ref[...], sc_ref[...], sh_ref[...]).astype(bf16)

    o_ref[...] = jnp.dot(h_ref[...], w_ref[...], preferred_element_type=f32)


def _norm_mod_kernel(x_ref, g_ref, sc_ref, sh_ref, o_ref):
    o_ref[...] = _norm_mod(x_ref[...], g_ref[...], sc_ref[...], sh_ref[...]).astype(bf16)


def norm_modulate(x, gain, mod, sc_blk, sh_blk, tm):
    m, d = x.shape
    assert mod.shape[0] == 1
    return pl.pallas_call(
        _norm_mod_kernel,
        grid=(m // tm,),
        in_specs=[pl.BlockSpec((tm, d), lambda i: (i, 0)),
                  pl.BlockSpec((1, d), lambda i: (0, 0)),
                  pl.BlockSpec((1, d), lambda i: (0, sc_blk)),
                  pl.BlockSpec((1, d), lambda i: (0, sh_blk))],
        out_specs=pl.BlockSpec((tm, d), lambda i: (i, 0)),
        out_shape=jax.ShapeDtypeStruct((m, d), bf16),
        compiler_params=_cparams("arbitrary"),
        name="norm_modulate",
    )(x, gain.reshape(1, d), mod, mod)


def _matmul_split_kernel(h_ref, w_ref, *o_refs, tile_ranges):
    j = pl.program_id(1)
    y = jnp.dot(h_ref[...], w_ref[...], preferred_element_type=f32)
    for o_ref, (lo, hi) in zip(o_refs, tile_ranges):
        @pl.when((j >= lo) & (j < hi))
        def _(o_ref=o_ref):
            o_ref[...] = y


def matmul_split(h, w, layer, tm, tn, widths):
    m, d = h.shape
    assert sum(widths) == w.shape[2] and all(wd % tn == 0 for wd in widths)
    bounds = np.cumsum([0] + [wd // tn for wd in widths])
    tile_ranges = [(int(lo), int(hi)) for lo, hi in zip(bounds[:-1], bounds[1:])]
    out_spec = lambda lo, hi: pl.BlockSpec((tm, tn), lambda i, j: (i, jnp.clip(j - lo, 0, hi - lo - 1)))
    return pl.pallas_call(
        functools.partial(_matmul_split_kernel, tile_ranges=tile_ranges),
        grid=(m // tm, w.shape[2] // tn),
        in_specs=[pl.BlockSpec((tm, d), lambda i, j: (i, 0)),
                  pl.BlockSpec((None, d, tn), lambda i, j: (layer, 0, j))],
        out_specs=[out_spec(lo, hi) for lo, hi in tile_ranges],
        out_shape=[jax.ShapeDtypeStruct((m, wd), f32) for wd in widths],
        compiler_params=_cparams("arbitrary", "arbitrary"),
        name="matmul_split",
    )(h, w)


def norm_mod_matmul(x, gain, mod, sc_blk, sh_blk, w, layer, tm, tn):
    m, d = x.shape
    n = w.shape[2]
    per_row = mod.shape[0] != 1
    mrows = tm if per_row else 1
    mod_spec = lambda blk: pl.BlockSpec((mrows, d), (lambda i, j: (i, blk)) if per_row else (lambda i, j: (0, blk)))
    return pl.pallas_call(
        _nmm_kernel,
        grid=(m // tm, n // tn),
        in_specs=[pl.BlockSpec((tm, d), lambda i, j: (i, 0)),
                  pl.BlockSpec((1, d), lambda i, j: (0, 0)),
                  mod_spec(sc_blk), mod_spec(sh_blk),
                  pl.BlockSpec((None, d, tn), lambda i, j: (layer, 0, j))],
        out_specs=pl.BlockSpec((tm, tn), lambda i, j: (i, j)),
        out_shape=jax.ShapeDtypeStruct((m, n), f32),
        scratch_shapes=[pltpu.VMEM((tm, d), bf16)],
        compiler_params=_cparams("arbitrary", "arbitrary"),
        name="norm_mod_matmul",
    )(x, gain.reshape(1, d), mod, mod, w)


def _tri_inverse(lmat, blk):
    n = lmat.shape[0]
    row = lax.broadcasted_iota(jnp.int32, (n, n), 0)
    col = lax.broadcasted_iota(jnp.int32, (n, n), 1)
    eye = jnp.where(row == col, 1.0, 0.0).astype(f32)
    diag = jnp.where(row // SUBLANES == col // SUBLANES, lmat, 0.0)
    d2 = _dot3(diag, diag)
    d4 = _dot3(d2, d2)
    t = _dot3(_dot3(eye - diag, eye + d2), eye + d4)
    b = SUBLANES
    while b < blk:
        off = jnp.where((row // (2 * b) == col // (2 * b)) & (row // b != col // b), lmat, 0.0)
        t = t - _dot3(_dot3(t, off), t)
        b *= 2
    return t


def _cumsum_rows(x):
    row = lax.broadcasted_iota(jnp.int32, x.shape, 0)
    s = 1
    while s < x.shape[0]:
        x = x + jnp.where(row >= s, pltpu.roll(x, s, 0), 0.0)
        s *= 2
    return x


def _gdn_kernel(qkv_ref, z_ref, ab_ref, hist_ref, s0_ref, cw_ref, alog_ref, dtb_ref, nw_ref,
                o_ref, sfin_ref, s_scr, carry, *, chunk, n_valid):
    c_idx = pl.program_id(1)
    n_seq, rows, _ = qkv_ref.shape
    c = chunk
    n_sub = rows // c
    per_group = MXU_DIM // (H_GDN * c)
    units = [(s, u) for s in range(n_seq) for u in range(n_sub)]
    assert len(units) % per_group == 0

    @pl.when(c_idx == 0)
    def _():
        s_scr[...] = s0_ref[...]
        carry[...] = hist_ref[...]

    rowid = lax.broadcasted_iota(jnp.int32, (c, LANES), 0)
    valid = rowid < n_valid
    nw = nw_ref[...]

    prep = {}
    for s in range(n_seq):
        raw = qkv_ref[s]
        ext = jnp.concatenate([carry[s], raw], axis=0)
        carry[s] = raw[rows - SUBLANES:, :]
        conv = ext[SUBLANES:] * cw_ref[GDN_CONV - 1:GDN_CONV, :]
        for i in range(1, GDN_CONV):
            conv = conv + pltpu.roll(ext, i, 0)[SUBLANES:] * cw_ref[GDN_CONV - 1 - i:GDN_CONV - i, :]
        qkv_all = _silu(conv)
        for u in range(n_sub):
            qkv = qkv_all[u * c:(u + 1) * c]
            ab = ab_ref[s, u * c:(u + 1) * c, :]
            x = ab + dtb_ref[...]
            sp = jnp.maximum(x, 0.0) + jnp.log1p(jnp.exp(-jnp.abs(x)))
            g_all = jnp.where(valid, -jnp.exp(alog_ref[...]) * sp, 0.0)
            beta_all = jnp.where(valid, 1.0 / (1.0 + jnp.exp(-ab)), 0.0)
            gc_all = _cumsum_rows(g_all)
            heads = []
            for h in range(H_GDN):
                q = qkv[:, h * HEAD_DIM:(h + 1) * HEAD_DIM]
                k = qkv[:, W_GDN + h * HEAD_DIM:W_GDN + (h + 1) * HEAD_DIM]
                v = qkv[:, 2 * W_GDN + h * HEAD_DIM:2 * W_GDN + (h + 1) * HEAD_DIM]
                q = q * lax.rsqrt(jnp.sum(q * q, axis=-1, keepdims=True) + 1e-6) * HEAD_DIM ** -0.5
                k = k * lax.rsqrt(jnp.sum(k * k, axis=-1, keepdims=True) + 1e-6)
                beta = beta_all[:, H_GDN + h:H_GDN + h + 1]
                gc = gc_all[:, h:h + 1]
                g_last = gc_all[c - 1:c, h:h + 1]
                heads.append(dict(q=q, k=k, kb=k * beta, vb=v * beta, gc=gc, g_last=g_last))
            prep[(s, u)] = heads

    n = MXU_DIM
    row = lax.broadcasted_iota(jnp.int32, (n, n), 0)
    col = lax.broadcasted_iota(jnp.int32, (n, n), 1)
    same = row // c == col // c
    lower = same & (row >= col)
    strict = same & (row > col)

    groups = []
    for g in range(len(units) // per_group):
        pieces = [(su, h) for su in units[g * per_group:(g + 1) * per_group] for h in range(H_GDN)]
        cat = lambda key: jnp.concatenate([prep[su][h][key] for su, h in pieces], axis=0)
        k_all, kb_all, gcol = cat("k"), cat("kb"), cat("gc")
        grow = jnp.broadcast_to(gcol, (n, LANES)).T[0:1, :]
        decay = jnp.where(lower, jnp.exp(jnp.where(lower, gcol - grow, 0.0)), 0.0)
        lmat = jnp.where(strict, _dot_nt(kb_all, k_all) * decay, 0.0)
        tinv = _tri_inverse(lmat, c)
        rhs = jnp.concatenate([cat("vb"), kb_all * jnp.exp(gcol)], axis=1)
        uw = _dot(tinv, rhs)
        qk = _dot_nt(cat("q"), k_all) * decay
        groups.append((pieces, uw, qk))

    for pieces, uw, qk in groups:
        v_new = []
        for j, ((s, u), h) in enumerate(pieces):
            blk = uw[j * c:(j + 1) * c]
            v_new.append(blk[:, :HEAD_DIM] - _dot(blk[:, HEAD_DIM:], s_scr[s, h]))
        o_intra = _dot(qk, jnp.concatenate(v_new, axis=0))
        for j, ((s, u), h) in enumerate(pieces):
            p = prep[(s, u)][h]
            st = s_scr[s, h]
            o = _dot(p["q"] * jnp.exp(p["gc"]), st) + o_intra[j * c:(j + 1) * c]
            s_scr[s, h] = st * jnp.exp(p["g_last"]) + _dot_tn(p["k"] * jnp.exp(p["g_last"] - p["gc"]), v_new[j])
            o = o * lax.rsqrt(jnp.mean(o * o, axis=-1, keepdims=True) + RMS_EPS) * nw
            sl = slice(h * HEAD_DIM, (h + 1) * HEAD_DIM)
            o_ref[s, u * c:(u + 1) * c, sl] = o * _silu(z_ref[s, u * c:(u + 1) * c, sl])

    @pl.when(c_idx == pl.num_programs(1) - 1)
    def _():
        sfin_ref[...] = s_scr[...]


def gdn_mixer(qkv, z, ab, hist, s0, conv_w, a_log, dt_bias, norm_w, chunk, n_valid, n_seq, n_sub):
    s_n, t, _ = qkv[0].shape
    rows = n_sub * chunk
    assert s_n % n_seq == 0 and t % rows == 0 and (n_seq * n_sub * H_GDN * chunk) % MXU_DIM == 0
    lane_pad = lambda a: jnp.zeros((1, LANES), f32).at[0, :H_GDN].set(a)
    return pl.pallas_call(
        functools.partial(_gdn_kernel, chunk=chunk, n_valid=n_valid),
        grid=(s_n // n_seq, t // rows),
        in_specs=[pl.BlockSpec((n_seq, rows, 3 * W_GDN), lambda s, c: (s, c, qkv[1])),
                  pl.BlockSpec((n_seq, rows, W_GDN), lambda s, c: (s, c, z[1])),
                  pl.BlockSpec((n_seq, rows, LANES), lambda s, c: (s, c, ab[1])),
                  pl.BlockSpec((n_seq, SUBLANES, 3 * W_GDN), lambda s, c: (s, 0, 0)),
                  pl.BlockSpec((None, n_seq, H_GDN, HEAD_DIM, HEAD_DIM), lambda s, c: (s0[1], s, 0, 0, 0)),
                  pl.BlockSpec((GDN_CONV, 3 * W_GDN), lambda s, c: (0, 0)),
                  pl.BlockSpec((1, LANES), lambda s, c: (0, 0)),
                  pl.BlockSpec((1, LANES), lambda s, c: (0, 0)),
                  pl.BlockSpec((1, HEAD_DIM), lambda s, c: (0, 0))],
        out_specs=[pl.BlockSpec((n_seq, rows, W_GDN), lambda s, c: (s, c, 0)),
                   pl.BlockSpec((n_seq, H_GDN, HEAD_DIM, HEAD_DIM), lambda s, c: (s, 0, 0, 0))],
        out_shape=[jax.ShapeDtypeStruct((s_n, t, W_GDN), f32),
                   jax.ShapeDtypeStruct((s_n, H_GDN, HEAD_DIM, HEAD_DIM), f32)],
        scratch_shapes=[pltpu.VMEM((n_seq, H_GDN, HEAD_DIM, HEAD_DIM), f32),
                        pltpu.VMEM((n_seq, SUBLANES, 3 * W_GDN), f32)],
        compiler_params=_cparams("arbitrary", "arbitrary"),
        name="gdn_mixer",
    )(qkv[0], z[0], ab[0], hist, s0[0], conv_w, lane_pad(a_log), lane_pad(dt_bias), norm_w.reshape(1, HEAD_DIM))


def _ret_kernel(q_ref, k_ref, v_ref, g_ref, cs_ref, sn_ref, s0_ref, o_ref, sfin_ref, s_scr, *, n_valid):
    c_idx = pl.program_id(1)
    n_seq, c, _ = q_ref.shape
    c_len = min(c, n_valid)

    @pl.when(c_idx == 0)
    def _():
        s_scr[...] = s0_ref[...]

    cs = cs_ref[...]
    sn = sn_ref[...]
    rot = lambda x: x * cs + pltpu.roll(x, HEAD_DIM // 2, 1) * sn
    r2 = lax.broadcasted_iota(jnp.int32, (c, c), 0)
    c2 = lax.broadcasted_iota(jnp.int32, (c, c), 1)
    rel = (r2 - c2).astype(f32)
    idx = lax.broadcasted_iota(jnp.int32, (c, 1), 0)
    idx_f = idx.astype(f32)
    valid = idx < n_valid

    for h in range(H_RET):
        lg = math.log(1.0 - 2.0 ** (-5.0 - h))
        sl = slice(h * HEAD_DIM, (h + 1) * HEAD_DIM)
        dmat = jnp.where(rel >= 0, jnp.exp(jnp.maximum(rel, 0.0) * lg), 0.0)
        xi = jnp.exp((idx_f + 1.0) * lg)
        zeta = jnp.where(valid, jnp.exp((c_len - 1.0 - idx_f) * lg), 0.0)
        for s in range(n_seq):
            q = rot(q_ref[s, :, sl])
            k = rot(k_ref[s, :, sl]) * HEAD_DIM ** -0.5
            v = v_ref[s, :, sl]
            st = s_scr[s, h]
            o = _dot(_dot_nt(q, k) * dmat, v) + _dot(q * xi, st)
            s_scr[s, h] = math.exp(c_len * lg) * st + _dot_tn(k * zeta, v)
            o = o * lax.rsqrt(jnp.mean(o * o, axis=-1, keepdims=True) + RMS_EPS)
            o_ref[s, :, sl] = o * _silu(g_ref[s, :, sl])

    @pl.when(c_idx == pl.num_programs(1) - 1)
    def _():
        sfin_ref[...] = s_scr[...]


def retention_mixer(qkvg, cs, sn, s0, chunk, n_valid, n_seq):
    proj, base = qkvg
    s_n, t, _ = proj.shape
    assert s_n % n_seq == 0 and t % chunk == 0
    col_spec = lambda part: pl.BlockSpec((n_seq, chunk, W_RET), lambda s, c: (s, c, base + part))
    return pl.pallas_call(
        functools.partial(_ret_kernel, n_valid=n_valid),
        grid=(s_n // n_seq, t // chunk),
        in_specs=[col_spec(0), col_spec(1), col_spec(2), col_spec(3),
                  pl.BlockSpec((chunk, HEAD_DIM), lambda s, c: (c, 0)),
                  pl.BlockSpec((chunk, HEAD_DIM), lambda s, c: (c, 0)),
                  pl.BlockSpec((None, n_seq, H_RET, HEAD_DIM, HEAD_DIM), lambda s, c: (s0[1], s, 0, 0, 0))],
        out_specs=[pl.BlockSpec((n_seq, chunk, W_RET), lambda s, c: (s, c, 0)),
                   pl.BlockSpec((n_seq, H_RET, HEAD_DIM, HEAD_DIM), lambda s, c: (s, 0, 0, 0))],
        out_shape=[jax.ShapeDtypeStruct((s_n, t, W_RET), f32),
                   jax.ShapeDtypeStruct((s_n, H_RET, HEAD_DIM, HEAD_DIM), f32)],
        scratch_shapes=[pltpu.VMEM((n_seq, H_RET, HEAD_DIM, HEAD_DIM), f32)],
        compiler_params=_cparams("arbitrary", "arbitrary"),
        name="retention_mixer",
    )(proj, proj, proj, proj, cs, sn, s0[0])


def _rotary_tables(pos):
    half = HEAD_DIM // 2
    inv = 1.0 / (10000.0 ** jnp.linspace(0.0, 1.0, half, dtype=f32))
    ang = pos.astype(f32)[:, None] * inv[None, :]
    cos, sin = jnp.cos(ang), jnp.sin(ang)
    return jnp.concatenate([cos, cos], axis=1), jnp.concatenate([-sin, sin], axis=1)


def _t5_bias_shifted(dist, tab):
    val = tab(0)
    for b in range(1, NUM_BUCKETS):
        val = jnp.where(dist >= _BUCKET_START[b], tab(b), val)
    return val - tab(NUM_BUCKETS - 1)


def _top_blocks(gate, n_sel):
    lane = lax.broadcasted_iota(jnp.int32, gate.shape, 1)
    sel = jnp.zeros(gate.shape, f32)
    for _ in range(n_sel):
        m = jnp.max(gate, axis=1, keepdims=True)
        first = jnp.min(jnp.where(gate == m, lane, gate.shape[1]), axis=1, keepdims=True)
        pick = (lane == first) & (m > 0.5 * NEG_INF)
        sel = jnp.where(pick, 1.0, sel)
        gate = jnp.where(pick, NEG_INF, gate)
    return sel


def _scores(q_aug, k_aug):
    return lax.dot_general(q_aug, k_aug, (((1,), (1,)), ((), ())), preferred_element_type=f32)


def _moba_prompt_kernel(tab_ref, q_ref, k_ref, v_ref, o_ref,
                        kaug_scr, v_scr, kmean_scr, bias_scr, s_scr, l_scr, acc_scr):
    h = pl.program_id(0)
    i = pl.program_id(1)
    blk = MOBA_BLOCK
    t = k_ref.shape[0]
    n_blocks = t // blk
    ck = FAR_CHUNK * blk
    masked_lane = LANES - 1

    @pl.when(i == 0)
    def _():
        k = k_ref[...]
        kmean_scr[...] = jnp.zeros(kmean_scr.shape, f32)
        kmean_scr[0:n_blocks, :] = jnp.mean(k.reshape(n_blocks, blk, HEAD_DIM), axis=1)
        row = lax.broadcasted_iota(jnp.int32, (t, LANES), 0)
        lane_t = lax.broadcasted_iota(jnp.int32, (t, LANES), 1)
        kaug_scr[blk:blk + t, 0:HEAD_DIM] = k.astype(bf16)
        kaug_scr[blk:blk + t, HEAD_DIM:] = jnp.where(lane_t == row // blk, 1.0, 0.0).astype(bf16)
        v_scr[blk:blk + t, :] = v_ref[...].astype(bf16)
        for start, size in ((0, blk), (blk + t, kaug_scr.shape[0] - blk - t)):
            if not size:
                continue
            lane_p = lax.broadcasted_iota(jnp.int32, (size, LANES), 1)
            kaug_scr[start:start + size, 0:HEAD_DIM] = jnp.zeros((size, HEAD_DIM), bf16)
            kaug_scr[start:start + size, HEAD_DIM:] = jnp.where(lane_p == masked_lane, 1.0, 0.0).astype(bf16)
            v_scr[start:start + size, :] = jnp.zeros((size, HEAD_DIM), bf16)
        r2 = lax.broadcasted_iota(jnp.int32, (blk, blk), 0)
        c2 = lax.broadcasted_iota(jnp.int32, (blk, blk), 1)
        tab = lambda b: tab_ref[b, h]
        bias_scr[0] = _t5_bias_shifted(r2 - c2 + blk, tab)
        bias_scr[1] = jnp.where(r2 >= c2, _t5_bias_shifted(r2 - c2, tab), NEG_INF)

    q = q_ref[...]
    lane = lax.broadcasted_iota(jnp.int32, (blk, LANES), 1)
    gate = jnp.where(lane < i, _dot3_nt(q, kmean_scr[...]), NEG_INF)
    sel = _top_blocks(gate, MOBA_TOPK) > 0.5
    qs = q * HEAD_DIM ** -0.5
    neg_far = jnp.where(sel & (lane < i - 1), 0.0, NEG_INF)
    neg_near = jnp.where((sel & (lane == i - 1)) | (lane == i), 0.0, NEG_INF)
    q_far = jnp.concatenate([qs, neg_far], axis=1).astype(bf16)
    q_near = jnp.concatenate([qs, neg_near], axis=1).astype(bf16)
    n_chunks = (jnp.maximum(i - 1, 0) + FAR_CHUNK - 1) // FAR_CHUNK

    far_rows = lambda c: pl.ds(pl.multiple_of(blk + c * ck, blk), ck)

    def scores_pass(c, mx):
        s = _scores(q_far, kaug_scr[far_rows(c), :])
        s_scr[c] = s
        return jnp.maximum(mx, _fold(s, jnp.maximum))

    mx = lax.fori_loop(0, n_chunks, scores_pass, jnp.full((blk, LANES), NEG_INF, f32))
    near_rows = pl.ds(pl.multiple_of(i * blk, blk), 2 * blk)
    s_near = _scores(q_near, kaug_scr[near_rows, :]) + jnp.concatenate([bias_scr[0], bias_scr[1]], axis=1)
    m = jnp.max(jnp.maximum(mx, _fold(s_near, jnp.maximum)), axis=1, keepdims=True)
    p_near = jnp.exp(s_near - m)
    l_scr[...] = _fold(p_near, jnp.add)
    acc_scr[...] = jnp.dot(p_near.astype(bf16), v_scr[near_rows, :], preferred_element_type=f32)

    def values_pass(c, carry):
        p = jnp.exp(s_scr[c] - m)
        l_scr[...] += _fold(p, jnp.add)
        acc_scr[...] += jnp.dot(p.astype(bf16), v_scr[far_rows(c), :], preferred_element_type=f32)
        return carry

    lax.fori_loop(0, n_chunks, values_pass, 0)
    o_ref[...] = acc_scr[...] / jnp.sum(l_scr[...], axis=1, keepdims=True)


def moba_prompt(q, k, v, rel_bias):
    t = q[0].shape[0]
    blk = MOBA_BLOCK
    n_blocks = t // blk
    assert t % blk == 0 and n_blocks < LANES - 1 and blk >= _BUCKET_START[-1]
    qb, kb, vb = q[1], k[1], v[1]
    far_max = -(-max(n_blocks - 2, 1) // FAR_CHUNK) * FAR_CHUNK
    kv_rows = (1 + max(n_blocks, far_max + 1)) * blk
    return pl.pallas_call(
        _moba_prompt_kernel,
        grid=(H_MOBA, n_blocks),
        in_specs=[pl.BlockSpec(memory_space=pltpu.SMEM),
                  pl.BlockSpec((blk, HEAD_DIM), lambda h, i: (i, qb + h)),
                  pl.BlockSpec((t, HEAD_DIM), lambda h, i: (0, kb + h)),
                  pl.BlockSpec((t, HEAD_DIM), lambda h, i: (0, vb + h))],
        out_specs=pl.BlockSpec((blk, HEAD_DIM), lambda h, i: (i, h)),
        out_shape=jax.ShapeDtypeStruct((t, W_MOBA), f32),
        scratch_shapes=[pltpu.VMEM((kv_rows, 2 * HEAD_DIM), bf16),
                        pltpu.VMEM((kv_rows, HEAD_DIM), bf16),
                        pltpu.VMEM((LANES, HEAD_DIM), f32),
                        pltpu.VMEM((2, blk, blk), f32),
                        pltpu.VMEM((far_max // FAR_CHUNK, blk, FAR_CHUNK * blk), f32),
                        pltpu.VMEM((blk, LANES), f32),
                        pltpu.VMEM((blk, HEAD_DIM), f32)],
        compiler_params=_cparams("arbitrary", "arbitrary"),
        name="moba_prompt",
    )(rel_bias, q[0], k[0], v[0])


def _moba_sample_kernel(pt_ref, q_ref, kn_ref, vn_ref, tabr_ref, *rest, n_pages, n_new):
    kp = rest[:n_pages]
    vp = rest[n_pages:2 * n_pages]
    o_ref, onehot_scr, near_scr, own_scr, s_scr = rest[2 * n_pages:]
    page = kp[0].shape[0]
    pr = page * H_MOBA
    past = n_pages * page
    ppb = MOBA_BLOCK // page
    n_blocks = n_pages // ppb
    rows = H_MOBA * n_new
    n_own = kn_ref.shape[0]
    near = [p for p in range(n_pages) if past - (p * page + page - 1) < _BUCKET_START[-1]]

    r_l = lax.broadcasted_iota(jnp.int32, (rows, LANES), 0)
    l_l = lax.broadcasted_iota(jnp.int32, (rows, LANES), 1)
    head_r = r_l // n_new
    t_r = r_l % n_new

    @pl.when(pl.program_id(0) == 0)
    def _():
        kr = lax.broadcasted_iota(jnp.int32, (pr, LANES), 0)
        kl = lax.broadcasted_iota(jnp.int32, (pr, LANES), 1)
        for n in range(n_blocks):
            onehot_scr[n] = jnp.where(kl == n * H_MOBA + kr % H_MOBA, 1.0, 0.0).astype(bf16)
        tabr = tabr_ref[...]
        tab = lambda b: tabr[:, b:b + 1]
        rr = lax.broadcasted_iota(jnp.int32, (rows, pr), 0)
        cc = lax.broadcasted_iota(jnp.int32, (rows, pr), 1)
        for idx, p in enumerate(near):
            near_scr[idx] = _t5_bias_shifted(past + rr % n_new - (p * page + cc // H_MOBA), tab)
        t_k = l_l // H_MOBA
        visible = (l_l % H_MOBA == head_r) & (t_k <= t_r) & (t_k < n_new)
        own_scr[...] = jnp.where(visible, _t5_bias_shifted(t_r - t_k, tab), NEG_INF)

    q = q_ref[...]
    kmean = []
    for n in range(n_blocks):
        tot = functools.reduce(jnp.add, [jnp.sum(kp[p][...], axis=0) for p in range(n * ppb, (n + 1) * ppb)])
        kmean.append(tot * (1.0 / MOBA_BLOCK))
    kmean.append(jnp.zeros((LANES - n_blocks * H_MOBA, HEAD_DIM), f32))
    gate = _dot3_nt(q, jnp.concatenate(kmean, axis=0))
    own_head = (l_l < n_blocks * H_MOBA) & (l_l % H_MOBA == head_r)
    sel = _top_blocks(jnp.where(own_head, gate, NEG_INF), min(MOBA_TOPK, n_blocks))
    qs = q * HEAD_DIM ** -0.5
    q_aug = jnp.concatenate([qs, jnp.where(sel > 0.5, 0.0, NEG_INF)], axis=1).astype(bf16)

    mx = jnp.full((rows, LANES), NEG_INF, f32)
    for p in range(n_pages):
        k_aug = jnp.concatenate([kp[p][...].reshape(pr, HEAD_DIM).astype(bf16), onehot_scr[p // ppb]], axis=1)
        s = _scores(q_aug, k_aug)
        if p in near:
            s = s + near_scr[near.index(p)]
        s_scr[:, p * pr:(p + 1) * pr] = s
        mx = jnp.maximum(mx, _fold(s, jnp.maximum))
    pad_own = lambda r: jnp.concatenate([r[...], jnp.zeros((LANES - n_own, HEAD_DIM), f32)], axis=0)
    s_own = _dot_nt(qs, pad_own(kn_ref)) + own_scr[...]
    m = jnp.max(jnp.maximum(mx, s_own), axis=1, keepdims=True)
    p_own = jnp.exp(s_own - m)
    lsum = p_own
    acc = _dot(p_own, pad_own(vn_ref))
    for p in range(n_pages):
        pe = jnp.exp(s_scr[:, p * pr:(p + 1) * pr] - m)
        lsum = lsum + _fold(pe, jnp.add)
        acc = acc + _dot(pe, vp[p][...].reshape(pr, HEAD_DIM))
    o_ref[...] = acc / jnp.sum(lsum, axis=1, keepdims=True)


def moba_sample(q, k_new, v_new, cache_k, cache_v, page_table, rel_bias, layer, n_new):
    bsz, rows, _ = q.shape
    n_pages = page_table.shape[1]
    page = cache_k.shape[2]
    past = n_pages * page
    n_blocks = past // MOBA_BLOCK
    assert past % MOBA_BLOCK == 0 and MOBA_BLOCK % page == 0 and n_blocks * H_MOBA <= LANES
    assert rows == H_MOBA * n_new and k_new.shape[1] <= LANES
    n_near = len([p for p in range(n_pages) if past - (p * page + page - 1) < _BUCKET_START[-1]])
    tab_rows = jnp.pad(jnp.repeat(rel_bias.T, n_new, axis=0), ((0, 0), (0, LANES - NUM_BUCKETS)))
    page_spec = lambda p: pl.BlockSpec((None, None, page, H_MOBA, HEAD_DIM),
                                       lambda b, pt: (layer, pt[b, p], 0, 0, 0))
    row_spec = lambda a: pl.BlockSpec((None,) + a.shape[1:], lambda b, pt: (b, 0, 0))
    grid_spec = pltpu.PrefetchScalarGridSpec(
        num_scalar_prefetch=1,
        grid=(bsz,),
        in_specs=([row_spec(q), row_spec(k_new), row_spec(v_new),
                   pl.BlockSpec(tab_rows.shape, lambda b, pt: (0, 0))]
                  + [page_spec(p) for p in range(n_pages)] * 2),
        out_specs=row_spec(q),
        scratch_shapes=[pltpu.VMEM((n_blocks, page * H_MOBA, LANES), bf16),
                        pltpu.VMEM((max(n_near, 1), rows, page * H_MOBA), f32),
                        pltpu.VMEM((rows, LANES), f32),
                        pltpu.VMEM((rows, past * H_MOBA), f32)],
    )
    return pl.pallas_call(
        functools.partial(_moba_sample_kernel, n_pages=n_pages, n_new=n_new),
        grid_spec=grid_spec,
        out_shape=jax.ShapeDtypeStruct(q.shape, f32),
        compiler_params=_cparams("arbitrary"),
        name="moba_sample",
    )(page_table, q, k_new, v_new, tab_rows, *([cache_k] * n_pages), *([cache_v] * n_pages))


def _post(x, y, gain, gate):
    return x + gate * (y * lax.rsqrt(jnp.mean(y * y, axis=-1, keepdims=True) + RMS_EPS) * gain)


def _out_proj_kernel(oa_ref, ob_ref, oc_ref, w_ref, x_ref, gt_ref, g_ref, o_ref):
    y = (jnp.dot(oa_ref[...].astype(bf16), w_ref[0:W_GDN, :], preferred_element_type=f32)
         + jnp.dot(ob_ref[...].astype(bf16), w_ref[W_GDN:W_GDN + W_MOBA, :], preferred_element_type=f32)
         + jnp.dot(oc_ref[...].astype(bf16), w_ref[W_GDN + W_MOBA:, :], preferred_element_type=f32))
    o_ref[...] = _post(x_ref[...], y, g_ref[...], gt_ref[...])


def out_proj_residual(o_a, o_b, o_c, w_out, layer, x, mod, gt_blk, gain, tm):
    m, d = x.shape
    per_row = mod.shape[0] != 1
    mod_spec = pl.BlockSpec((tm if per_row else 1, d), (lambda i: (i, gt_blk)) if per_row else (lambda i: (0, gt_blk)))
    return pl.pallas_call(
        _out_proj_kernel,
        grid=(m // tm,),
        in_specs=[pl.BlockSpec((tm, W_GDN), lambda i: (i, 0)),
                  pl.BlockSpec((tm, W_MOBA), lambda i: (i, 0)),
                  pl.BlockSpec((tm, W_RET), lambda i: (i, 0)),
                  pl.BlockSpec((None,) + w_out.shape[1:], lambda i: (layer, 0, 0)),
                  pl.BlockSpec((tm, d), lambda i: (i, 0)),
                  mod_spec,
                  pl.BlockSpec((1, d), lambda i: (0, 0))],
        out_specs=pl.BlockSpec((tm, d), lambda i: (i, 0)),
        out_shape=jax.ShapeDtypeStruct((m, d), f32),
        compiler_params=_cparams("arbitrary"),
        name="out_proj_residual",
    )(o_a, o_b, o_c, w_out, x, mod, gain.reshape(1, d))


def _ffn_down_kernel(ug_ref, uu_ref, pg_ref, pu_ref, hg_ref, hu_ref, cwg_ref, cwu_ref, wd_ref,
                     x_ref, gt_ref, g_ref, o_ref, acc_ref, *, row_shift):
    i = pl.program_id(0)
    f = pl.program_id(1)
    hdr = hg_ref.shape[0]
    first = i == 0

    def conv(u_ref, p_ref, h_ref, cw_ref):
        head = jnp.where(first, h_ref[...], p_ref[...])
        ext = jnp.concatenate([head, u_ref[...]], axis=0)
        out = ext[hdr:] * cw_ref[FFN_CONV - 1:FFN_CONV, :]
        for j in range(1, FFN_CONV):
            out = out + pltpu.roll(ext, j * row_shift, 0)[hdr:] * cw_ref[FFN_CONV - 1 - j:FFN_CONV - j, :]
        return out

    act = _silu(conv(ug_ref, pg_ref, hg_ref, cwg_ref)) * conv(uu_ref, pu_ref, hu_ref, cwu_ref)
    part = jnp.dot(act.astype(bf16), wd_ref[...], preferred_element_type=f32)

    @pl.when(f == 0)
    def _():
        acc_ref[...] = part

    @pl.when(f > 0)
    def _():
        acc_ref[...] += part

    @pl.when(f == pl.num_programs(1) - 1)
    def _():
        o_ref[...] = _post(x_ref[...], acc_ref[...], g_ref[...], gt_ref[...])


def ffn_down_residual(u, hist, conv_w, w_down, layer, x, mod, gt_blk, gain, tm, tf, row_shift):
    m, d = x.shape
    ff = w_down.shape[1]
    nf = ff // tf
    hdr = hist.shape[0]
    assert hdr >= (FFN_CONV - 1) * row_shift and tm % hdr == 0
    per_row = mod.shape[0] != 1
    mod_spec = pl.BlockSpec((tm if per_row else 1, d),
                            (lambda i, f: (i, gt_blk)) if per_row else (lambda i, f: (0, gt_blk)))
    prev = lambda off: (lambda i, f: (jnp.maximum(i * (tm // hdr) - 1, 0), f + off))
    return pl.pallas_call(
        functools.partial(_ffn_down_kernel, row_shift=row_shift),
        grid=(m // tm, nf),
        in_specs=[pl.BlockSpec((tm, tf), lambda i, f: (i, f)),
                  pl.BlockSpec((tm, tf), lambda i, f: (i, f + nf)),
                  pl.BlockSpec((hdr, tf), prev(0)),
                  pl.BlockSpec((hdr, tf), prev(nf)),
                  pl.BlockSpec((hdr, tf), lambda i, f: (0, f)),
                  pl.BlockSpec((hdr, tf), lambda i, f: (0, f + nf)),
                  pl.BlockSpec((FFN_CONV, tf), lambda i, f: (0, f)),
                  pl.BlockSpec((FFN_CONV, tf), lambda i, f: (0, f + nf)),
                  pl.BlockSpec((None, tf, d), lambda i, f: (layer, f, 0)),
                  pl.BlockSpec((tm, d), lambda i, f: (i, 0)),
                  mod_spec,
                  pl.BlockSpec((1, d), lambda i, f: (0, 0))],
        out_specs=pl.BlockSpec((tm, d), lambda i, f: (i, 0)),
        out_shape=jax.ShapeDtypeStruct((m, d), f32),
        scratch_shapes=[pltpu.VMEM((tm, d), f32)],
        compiler_params=_cparams("arbitrary", "arbitrary"),
        name="ffn_down_residual",
    )(u, u, u, u, hist, hist, conv_w, conv_w, w_down, x, mod, gain.reshape(1, d))


def _permute_w_in(w_in):
    s1 = 3 * W_GDN
    s2 = s1 + W_GDN
    s4 = s2 + 2 * H_GDN
    ab = w_in[..., s2:s4]
    pad = jnp.zeros(w_in.shape[:-1] + (PROJ_W - COL_AB - 2 * H_GDN,), w_in.dtype)
    return jnp.concatenate([w_in[..., :s2], w_in[..., s4:], ab, pad], axis=-1).astype(bf16)


def kernel(x_prompt, x_sample, c_prompt, c_sample, cache_k, cache_v, page_table, state_gdn_conv, state_gdn, state_ret, state_ffn_conv, w_ada, b_ada, g_pre_mix, g_post_mix, g_pre_ffn, g_post_ffn, w_in, gdn_conv_w, gdn_a_log, gdn_dt_bias, gdn_norm_w, rel_bias, w_out, w_up, ffn_conv_w, w_down):
    bp, seq, d = x_prompt.shape
    bs, dec = x_sample.shape[:2]
    depth = w_in.shape[0]
    assert bp == 1 and dec <= SUBLANES
    past_len = page_table.shape[1] * cache_k.shape[2]
    d_ff = w_down.shape[1]
    pad_t = SUBLANES - dec
    seq_per_step = MXU_DIM // (H_GDN * SUBLANES)

    w_in_b = _permute_w_in(w_in)
    w_out_b = w_out.astype(bf16)
    w_up_b = w_up.astype(bf16)
    w_down_b = w_down.astype(bf16)

    n_c = bp + bs
    c_all = jnp.concatenate([c_prompt, c_sample, jnp.zeros((-n_c % SUBLANES, d), f32)], axis=0)
    mod = ada_modulation(c_all, w_ada, b_ada)

    cs_p, sn_p = _rotary_tables(jnp.arange(seq, dtype=jnp.int32))
    cs_s, sn_s = _rotary_tables(past_len + jnp.arange(SUBLANES, dtype=jnp.int32))

    yp = x_prompt.reshape(seq, d)
    ys = jnp.swapaxes(x_sample, 0, 1).reshape(dec * bs, d)
    to_bm = lambda a: jnp.pad(jnp.swapaxes(a.reshape(dec, bs, -1), 0, 1), ((0, 0), (0, pad_t), (0, 0)))
    to_tm = lambda a: jnp.swapaxes(a[:, :dec], 0, 1).reshape(dec * bs, -1)
    heads_tm = lambda a: a.reshape(dec, bs, H_MOBA, HEAD_DIM)

    assert H_GDN == H_RET
    zero_state = jnp.zeros((1, bp, H_GDN, HEAD_DIM, HEAD_DIM), f32)

    outs = [[] for _ in range(12)]
    for l in range(depth):
        mod_p = mod[l, 0:bp]
        mod_s = jnp.tile(mod[l, bp:bp + bs], (dec, 1))

        h_p = norm_modulate(yp, g_pre_mix[l], mod_p, 1, 0, tm=1024)
        qkva_p, z_p, qb_p, kb_p, vb_p, c_p, ab_p = matmul_split(
            h_p, w_in_b, l, tm=1024, tn=SPLIT_TILE,
            widths=(3 * W_GDN, W_GDN, W_MOBA, W_MOBA, W_MOBA, 4 * W_RET, PROJ_W - COL_AB))
        oa_p, gs_p = gdn_mixer((qkva_p[None], 0), (z_p[None], 0), (ab_p[None], 0),
                               jnp.zeros((1, SUBLANES, 3 * W_GDN), f32), (zero_state, 0),
                               gdn_conv_w[l], gdn_a_log[l], gdn_dt_bias[l], gdn_norm_w[l],
                               chunk=GDN_CHUNK, n_valid=GDN_CHUNK, n_seq=1, n_sub=2)
        oc_p, rs_p = retention_mixer((c_p[None], 0), cs_p, sn_p, (zero_state, 0),
                                     chunk=RET_CHUNK, n_valid=RET_CHUNK, n_seq=1)
        ob_p = moba_prompt((qb_p, 0), (kb_p, 0), (vb_p, 0), rel_bias)
        yp = out_proj_residual(oa_p[0], ob_p, oc_p[0], w_out_b, l, yp, mod_p, 2, g_post_mix[l], tm=512)

        proj_s = norm_mod_matmul(ys, g_pre_mix[l], mod_s, 1, 0, w_in_b, l, tm=dec * bs, tn=PROJ_TILE)
        proj_sb = to_bm(proj_s)
        hist_s = jnp.pad(state_gdn_conv[l], ((0, 0), (SUBLANES - (GDN_CONV - 1), 0), (0, 0)))
        oa_s, gs_s = gdn_mixer((proj_sb, COL_QKV_A // (3 * W_GDN)), (proj_sb, COL_Z_A // W_GDN),
                               (proj_sb, COL_AB // LANES), hist_s, (state_gdn, l), gdn_conv_w[l], gdn_a_log[l],
                               gdn_dt_bias[l], gdn_norm_w[l], chunk=SUBLANES, n_valid=dec, n_seq=seq_per_step,
                               n_sub=1)
        oc_s, rs_s = retention_mixer((proj_sb, COL_QKVG_C // W_RET), cs_s, sn_s, (state_ret, l), chunk=SUBLANES,
                                     n_valid=dec, n_seq=seq_per_step)
        q_s, k_s, v_s = [heads_tm(proj_s[:, COL_QKV_B + j * W_MOBA:COL_QKV_B + (j + 1) * W_MOBA]) for j in range(3)]
        k_s, v_s = jnp.swapaxes(k_s, 0, 1), jnp.swapaxes(v_s, 0, 1)
        ob_s = moba_sample(jnp.transpose(q_s, (1, 2, 0, 3)).reshape(bs, H_MOBA * dec, HEAD_DIM),
                           k_s.reshape(bs, dec * H_MOBA, HEAD_DIM), v_s.reshape(bs, dec * H_MOBA, HEAD_DIM),
                           cache_k, cache_v, page_table, rel_bias, l, dec)
        ob_s = jnp.transpose(ob_s.reshape(bs, H_MOBA, dec, HEAD_DIM), (2, 0, 1, 3)).reshape(dec * bs, W_MOBA)
        ys = out_proj_residual(to_tm(oa_s), ob_s, to_tm(oc_s), w_out_b, l, ys, mod_s, 2,
                               g_post_mix[l], tm=dec * bs)

        u_p = norm_mod_matmul(yp, g_pre_ffn[l], mod_p, 4, 3, w_up_b, l, tm=1024, tn=1024)
        yp = ffn_down_residual(u_p, jnp.zeros((SUBLANES, 2 * d_ff), f32), ffn_conv_w[l], w_down_b, l, yp,
                               mod_p, 5, g_post_ffn[l], tm=512, tf=d_ff // 4, row_shift=1)
        u_s = norm_mod_matmul(ys, g_pre_ffn[l], mod_s, 4, 3, w_up_b, l, tm=dec * bs, tn=1024)
        hist_f = jnp.swapaxes(state_ffn_conv[l], 0, 1).reshape((FFN_CONV - 1) * bs, 2 * d_ff)
        ys = ffn_down_residual(u_s, hist_f, ffn_conv_w[l], w_down_b, l, ys, mod_s, 5, g_post_ffn[l],
                               tm=dec * bs, tf=512, row_shift=bs)

        outs[0].append(kb_p.reshape(bp, seq, H_MOBA, HEAD_DIM))
        outs[1].append(vb_p.reshape(bp, seq, H_MOBA, HEAD_DIM))
        outs[2].append(k_s)
        outs[3].append(v_s)
        outs[4].append(qkva_p[seq - (GDN_CONV - 1):].reshape(bp, GDN_CONV - 1, 3 * W_GDN))
        outs[5].append(proj_sb[:, dec - (GDN_CONV - 1):dec, :3 * W_GDN])
        outs[6].append(gs_p)
        outs[7].append(gs_s)
        outs[8].append(rs_p)
        outs[9].append(rs_s)
        outs[10].append(u_p[seq - (FFN_CONV - 1):].reshape(bp, FFN_CONV - 1, 2 * d_ff))
        outs[11].append(jnp.swapaxes(u_s.reshape(dec, bs, 2 * d_ff)[dec - (FFN_CONV - 1):], 0, 1))

    y_prompt = yp.reshape(bp, seq, d)
    y_sample = jnp.swapaxes(ys.reshape(dec, bs, d), 0, 1)
    return (y_prompt, y_sample) + tuple(jnp.stack(o, axis=0) for o in outs)
```

```python
import functools
import math

import numpy as np
import jax
import jax.numpy as jnp
from jax import lax
from jax.experimental import pallas as pl
from jax.experimental.pallas import tpu as pltpu

f32 = jnp.float32
bf16 = jnp.bfloat16

HEAD_DIM = 128
H_GDN = 4
H_MOBA = 8
H_RET = 4
W_GDN = H_GDN * HEAD_DIM
W_MOBA = H_MOBA * HEAD_DIM
W_RET = H_RET * HEAD_DIM
GDN_CONV = 4
FFN_CONV = 3
MOBA_BLOCK = 256
MOBA_TOPK = 3
NUM_BUCKETS = 32
MAX_DISTANCE = 128
RMS_EPS = 1e-6
NEG_INF = -1e30

SUBLANES = 8
LANES = 128
MXU_DIM = 256
VMEM_LIMIT = 56 * 1024 * 1024

COL_QKV_A = 0
COL_Z_A = 3 * W_GDN
COL_QKV_B = COL_Z_A + W_GDN
COL_QKVG_C = COL_QKV_B + 3 * W_MOBA
COL_AB = COL_QKVG_C + 4 * W_RET
PROJ_TILE = 5 * MXU_DIM
PROJ_W = -(-(COL_AB + LANES) // PROJ_TILE) * PROJ_TILE

SPLIT_TILE = 2 * MXU_DIM
assert all(c % SPLIT_TILE == 0 for c in (COL_Z_A, COL_QKV_B, W_MOBA, COL_QKVG_C, COL_AB, PROJ_W))

GDN_CHUNK = 64
RET_CHUNK = 256
FAR_CHUNK = 4


def _t5_bucket_starts():
    max_exact = NUM_BUCKETS // 2
    d = np.arange(0, 4 * MAX_DISTANCE)
    dd = np.maximum(d, max_exact).astype(np.float64)
    large = max_exact + (np.log(dd / max_exact) / math.log(MAX_DISTANCE / max_exact)
                         * (NUM_BUCKETS - max_exact)).astype(np.int32)
    bucket = np.where(d < max_exact, d, np.minimum(large, NUM_BUCKETS - 1))
    return [int(np.argmax(bucket >= b)) for b in range(NUM_BUCKETS)]


_BUCKET_START = _t5_bucket_starts()


def _cparams(*sem):
    return pltpu.CompilerParams(dimension_semantics=sem, vmem_limit_bytes=VMEM_LIMIT)


def _silu(x):
    return x * (1.0 / (1.0 + jnp.exp(-x)))


def _dot(a, b):
    return jnp.dot(a.astype(bf16), b.astype(bf16), preferred_element_type=f32)


def _dot_nt(a, b):
    return lax.dot_general(a.astype(bf16), b.astype(bf16), (((1,), (1,)), ((), ())),
                           preferred_element_type=f32)


def _dot_tn(a, b):
    return lax.dot_general(a.astype(bf16), b.astype(bf16), (((0,), (0,)), ((), ())),
                           preferred_element_type=f32)


def _split(a):
    hi = a.astype(bf16)
    lo = (a - hi.astype(f32)).astype(bf16)
    return hi, lo


def _dot3(a, b):
    ah, al = _split(a)
    bh, bl = _split(b)
    d = lambda x, y: jnp.dot(x, y, preferred_element_type=f32)
    return d(ah, bh) + (d(ah, bl) + d(al, bh))


def _dot3_nt(a, b):
    ah, al = _split(a)
    bh, bl = _split(b)
    d = lambda x, y: lax.dot_general(x, y, (((1,), (1,)), ((), ())), preferred_element_type=f32)
    return d(ah, bh) + (d(ah, bl) + d(al, bh))


def _fold(x, op):
    return functools.reduce(op, [x[:, i * LANES:(i + 1) * LANES] for i in range(x.shape[1] // LANES)])


def _ada_kernel(c_ref, w_ref, b_ref, o_ref):
    o_ref[...] = _dot(_silu(c_ref[...]), w_ref[...]) + b_ref[...]


def ada_modulation(c_all, w_ada, b_ada, tn=1024):
    depth, d_model, n = w_ada.shape
    rows = c_all.shape[0]
    return pl.pallas_call(
        _ada_kernel,
        grid=(depth, n // tn),
        in_specs=[pl.BlockSpec((rows, d_model), lambda l, j: (0, 0)),
                  pl.BlockSpec((None, d_model, tn), lambda l, j: (l, 0, j)),
                  pl.BlockSpec((None, 1, tn), lambda l, j: (l, 0, j))],
        out_specs=pl.BlockSpec((None, rows, tn), lambda l, j: (l, 0, j)),
        out_shape=jax.ShapeDtypeStruct((depth, rows, n), f32),
        compiler_params=_cparams("arbitrary", "arbitrary"),
        name="ada_modulation",
    )(c_all, w_ada, b_ada.reshape(depth, 1, n))


def _norm_mod(x, gain, scale, shift):
    ms = jnp.mean(x * x, axis=-1, keepdims=True)
    return (x * lax.rsqrt(ms + RMS_EPS) * gain) * (1.0 + scale) + shift


def _nmm_kernel(x_ref, g_ref, sc_ref, sh_ref, w_ref, o_ref, h_ref):
    @pl.when(pl.program_id(1) == 0)
    def _():
        h_ref[...] = _norm_mod(x_ref[...], g_ref[...], sc_ref[...], sh_ref[...]).astype(bf16)

    o_ref[...] = jnp.dot(h_ref[...], w_ref[...], preferred_element_type=f32)


def _norm_mod_kernel(x_ref, g_ref, sc_ref, sh_ref, o_ref):
    o_ref[...] = _norm_mod(x_ref[...], g_ref[...], sc_ref[...], sh_ref[...]).astype(bf16)


def norm_modulate(x, gain, mod, sc_blk, sh_blk, tm):
    m, d = x.shape
    assert mod.shape[0] == 1
    return pl.pallas_call(
        _norm_mod_kernel,
        grid=(m // tm,),
        in_specs=[pl.BlockSpec((tm, d), lambda i: (i, 0)),
                  pl.BlockSpec((1, d), lambda i: (0, 0)),
                  pl.BlockSpec((1, d), lambda i: (0, sc_blk)),
                  pl.BlockSpec((1, d), lambda i: (0, sh_blk))],
        out_specs=pl.BlockSpec((tm, d), lambda i: (i, 0)),
        out_shape=jax.ShapeDtypeStruct((m, d), bf16),
        compiler_params=_cparams("arbitrary"),
        name="norm_modulate",
    )(x, gain.reshape(1, d), mod, mod)


def _matmul_split_kernel(h_ref, w_ref, *o_refs, tile_ranges):
    j = pl.program_id(1)
    y = jnp.dot(h_ref[...], w_ref[...], preferred_element_type=f32)
    for o_ref, (lo, hi) in zip(o_refs, tile_ranges):
        @pl.when((j >= lo) & (j < hi))
        def _(o_ref=o_ref):
            o_ref[...] = y


def matmul_split(h, w, layer, tm, tn, widths):
    m, d = h.shape
    assert sum(widths) == w.shape[2] and all(wd % tn == 0 for wd in widths)
    bounds = np.cumsum([0] + [wd // tn for wd in widths])
    tile_ranges = [(int(lo), int(hi)) for lo, hi in zip(bounds[:-1], bounds[1:])]
    out_spec = lambda lo, hi: pl.BlockSpec((tm, tn), lambda i, j: (i, jnp.clip(j - lo, 0, hi - lo - 1)))
    return pl.pallas_call(
        functools.partial(_matmul_split_kernel, tile_ranges=tile_ranges),
        grid=(m // tm, w.shape[2] // tn),
        in_specs=[pl.BlockSpec((tm, d), lambda i, j: (i, 0)),
                  pl.BlockSpec((None, d, tn), lambda i, j: (layer, 0, j))],
        out_specs=[out_spec(lo, hi) for lo, hi in tile_ranges],
        out_shape=[jax.ShapeDtypeStruct((m, wd), f32) for wd in widths],
        compiler_params=_cparams("arbitrary", "arbitrary"),
        name="matmul_split",
    )(h, w)


def norm_mod_matmul(x, gain, mod, sc_blk, sh_blk, w, layer, tm, tn):
    m, d = x.shape
    n = w.shape[2]
    per_row = mod.shape[0] != 1
    mrows = tm if per_row else 1
    mod_spec = lambda blk: pl.BlockSpec((mrows, d), (lambda i, j: (i, blk)) if per_row else (lambda i, j: (0, blk)))
    return pl.pallas_call(
        _nmm_kernel,
        grid=(m // tm, n // tn),
        in_specs=[pl.BlockSpec((tm, d), lambda i, j: (i, 0)),
                  pl.BlockSpec((1, d), lambda i, j: (0, 0)),
                  mod_spec(sc_blk), mod_spec(sh_blk),
                  pl.BlockSpec((None, d, tn), lambda i, j: (layer, 0, j))],
        out_specs=pl.BlockSpec((tm, tn), lambda i, j: (i, j)),
        out_shape=jax.ShapeDtypeStruct((m, n), f32),
        scratch_shapes=[pltpu.VMEM((tm, d), bf16)],
        compiler_params=_cparams("arbitrary", "arbitrary"),
        name="norm_mod_matmul",
    )(x, gain.reshape(1, d), mod, mod, w)


def _tri_inverse(lmat, blk):
    n = lmat.shape[0]
    row = lax.broadcasted_iota(jnp.int32, (n, n), 0)
    col = lax.broadcasted_iota(jnp.int32, (n, n), 1)
    eye = jnp.where(row == col, 1.0, 0.0).astype(f32)
    diag = jnp.where(row // SUBLANES == col // SUBLANES, lmat, 0.0)
    d2 = _dot3(diag, diag)
    d4 = _dot3(d2, d2)
    t = _dot3(_dot3(eye - diag, eye + d2), eye + d4)
    b = SUBLANES
    while b < blk:
        off = jnp.where((row // (2 * b) == col // (2 * b)) & (row // b != col // b), lmat, 0.0)
        t = t - _dot3(_dot3(t, off), t)
        b *= 2
    return t


def _cumsum_rows(x):
    row = lax.broadcasted_iota(jnp.int32, x.shape, 0)
    s = 1
    while s < x.shape[0]:
        x = x + jnp.where(row >= s, pltpu.roll(x, s, 0), 0.0)
        s *= 2
    return x


def _gdn_kernel(qkv_ref, z_ref, ab_ref, hist_ref, s0_ref, cw_ref, alog_ref, dtb_ref, nw_ref,
                o_ref, sfin_ref, s_scr, carry, *, chunk, n_valid):
    c_idx = pl.program_id(1)
    n_seq, rows, _ = qkv_ref.shape
    c = chunk
    n_sub = rows // c
    per_group = MXU_DIM // (H_GDN * c)
    units = [(s, u) for s in range(n_seq) for u in range(n_sub)]
    assert len(units) % per_group == 0

    @pl.when(c_idx == 0)
    def _():
        s_scr[...] = s0_ref[...]
        carry[...] = hist_ref[...]

    rowid = lax.broadcasted_iota(jnp.int32, (c, LANES), 0)
    valid = rowid < n_valid
    nw = nw_ref[...]

    prep = {}
    for s in range(n_seq):
        raw = qkv_ref[s]
        ext = jnp.concatenate([carry[s], raw], axis=0)
        carry[s] = raw[rows - SUBLANES:, :]
        conv = ext[SUBLANES:] * cw_ref[GDN_CONV - 1:GDN_CONV, :]
        for i in range(1, GDN_CONV):
            conv = conv + pltpu.roll(ext, i, 0)[SUBLANES:] * cw_ref[GDN_CONV - 1 - i:GDN_CONV - i, :]
        qkv_all = _silu(conv)
        for u in range(n_sub):
            qkv = qkv_all[u * c:(u + 1) * c]
            ab = ab_ref[s, u * c:(u + 1) * c, :]
            x = ab + dtb_ref[...]
            sp = jnp.maximum(x, 0.0) + jnp.log1p(jnp.exp(-jnp.abs(x)))
            g_all = jnp.where(valid, -jnp.exp(alog_ref[...]) * sp, 0.0)
            beta_all = jnp.where(valid, 1.0 / (1.0 + jnp.exp(-ab)), 0.0)
            gc_all = _cumsum_rows(g_all)
            heads = []
            for h in range(H_GDN):
                q = qkv[:, h * HEAD_DIM:(h + 1) * HEAD_DIM]
                k = qkv[:, W_GDN + h * HEAD_DIM:W_GDN + (h + 1) * HEAD_DIM]
                v = qkv[:, 2 * W_GDN + h * HEAD_DIM:2 * W_GDN + (h + 1) * HEAD_DIM]
                q = q * lax.rsqrt(jnp.sum(q * q, axis=-1, keepdims=True) + 1e-6) * HEAD_DIM ** -0.5
                k = k * lax.rsqrt(jnp.sum(k * k, axis=-1, keepdims=True) + 1e-6)
                beta = beta_all[:, H_GDN + h:H_GDN + h + 1]
                gc = gc_all[:, h:h + 1]
                g_last = gc_all[c - 1:c, h:h + 1]
                heads.append(dict(q=q, k=k, kb=k * beta, vb=v * beta, gc=gc, g_last=g_last))
            prep[(s, u)] = heads

    n = MXU_DIM
    row = lax.broadcasted_iota(jnp.int32, (n, n), 0)
    col = lax.broadcasted_iota(jnp.int32, (n, n), 1)
    same = row // c == col // c
    lower = same & (row >= col)
    strict = same & (row > col)

    groups = []
    for g in range(len(units) // per_group):
        pieces = [(su, h) for su in units[g * per_group:(g + 1) * per_group] for h in range(H_GDN)]
        cat = lambda key: jnp.concatenate([prep[su][h][key] for su, h in pieces], axis=0)
        k_all, kb_all, gcol = cat("k"), cat("kb"), cat("gc")
        grow = jnp.broadcast_to(gcol, (n, LANES)).T[0:1, :]
        decay = jnp.where(lower, jnp.exp(jnp.where(lower, gcol - grow, 0.0)), 0.0)
        lmat = jnp.where(strict, _dot_nt(kb_all, k_all) * decay, 0.0)
        tinv = _tri_inverse(lmat, c)
        rhs = jnp.concatenate([cat("vb"), kb_all * jnp.exp(gcol)], axis=1)
        uw = _dot(tinv, rhs)
        qk = _dot_nt(cat("q"), k_all) * decay
        groups.append((pieces, uw, qk))

    for pieces, uw, qk in groups:
        v_new = []
        for j, ((s, u), h) in enumerate(pieces):
            blk = uw[j * c:(j + 1) * c]
            v_new.append(blk[:, :HEAD_DIM] - _dot(blk[:, HEAD_DIM:], s_scr[s, h]))
        o_intra = _dot(qk, jnp.concatenate(v_new, axis=0))
        for j, ((s, u), h) in enumerate(pieces):
            p = prep[(s, u)][h]
            st = s_scr[s, h]
            o = _dot(p["q"] * jnp.exp(p["gc"]), st) + o_intra[j * c:(j + 1) * c]
            s_scr[s, h] = st * jnp.exp(p["g_last"]) + _dot_tn(p["k"] * jnp.exp(p["g_last"] - p["gc"]), v_new[j])
            o = o * lax.rsqrt(jnp.mean(o * o, axis=-1, keepdims=True) + RMS_EPS) * nw
            sl = slice(h * HEAD_DIM, (h + 1) * HEAD_DIM)
            o_ref[s, u * c:(u + 1) * c, sl] = o * _silu(z_ref[s, u * c:(u + 1) * c, sl])

    @pl.when(c_idx == pl.num_programs(1) - 1)
    def _():
        sfin_ref[...] = s_scr[...]


def gdn_mixer(qkv, z, ab, hist, s0, conv_w, a_log, dt_bias, norm_w, chunk, n_valid, n_seq, n_sub):
    s_n, t, _ = qkv[0].shape
    rows = n_sub * chunk
    assert s_n % n_seq == 0 and t % rows == 0 and (n_seq * n_sub * H_GDN * chunk) % MXU_DIM == 0
    lane_pad = lambda a: jnp.zeros((1, LANES), f32).at[0, :H_GDN].set(a)
    return pl.pallas_call(
        functools.partial(_gdn_kernel, chunk=chunk, n_valid=n_valid),
        grid=(s_n // n_seq, t // rows),
        in_specs=[pl.BlockSpec((n_seq, rows, 3 * W_GDN), lambda s, c: (s, c, qkv[1])),
                  pl.BlockSpec((n_seq, rows, W_GDN), lambda s, c: (s, c, z[1])),
                  pl.BlockSpec((n_seq, rows, LANES), lambda s, c: (s, c, ab[1])),
                  pl.BlockSpec((n_seq, SUBLANES, 3 * W_GDN), lambda s, c: (s, 0, 0)),
                  pl.BlockSpec((None, n_seq, H_GDN, HEAD_DIM, HEAD_DIM), lambda s, c: (s0[1], s, 0, 0, 0)),
                  pl.BlockSpec((GDN_CONV, 3 * W_GDN), lambda s, c: (0, 0)),
                  pl.BlockSpec((1, LANES), lambda s, c: (0, 0)),
                  pl.BlockSpec((1, LANES), lambda s, c: (0, 0)),
                  pl.BlockSpec((1, HEAD_DIM), lambda s, c: (0, 0))],
        out_specs=[pl.BlockSpec((n_seq, rows, W_GDN), lambda s, c: (s, c, 0)),
                   pl.BlockSpec((n_seq, H_GDN, HEAD_DIM, HEAD_DIM), lambda s, c: (s, 0, 0, 0))],
        out_shape=[jax.ShapeDtypeStruct((s_n, t, W_GDN), f32),
                   jax.ShapeDtypeStruct((s_n, H_GDN, HEAD_DIM, HEAD_DIM), f32)],
        scratch_shapes=[pltpu.VMEM((n_seq, H_GDN, HEAD_DIM, HEAD_DIM), f32),
                        pltpu.VMEM((n_seq, SUBLANES, 3 * W_GDN), f32)],
        compiler_params=_cparams("arbitrary", "arbitrary"),
        name="gdn_mixer",
    )(qkv[0], z[0], ab[0], hist, s0[0], conv_w, lane_pad(a_log), lane_pad(dt_bias), norm_w.reshape(1, HEAD_DIM))


def _ret_kernel(q_ref, k_ref, v_ref, g_ref, cs_ref, sn_ref, s0_ref, o_ref, sfin_ref, s_scr, *, n_valid):
    c_idx = pl.program_id(1)
    n_seq, c, _ = q_ref.shape
    c_len = min(c, n_valid)

    @pl.when(c_idx == 0)
    def _():
        s_scr[...] = s0_ref[...]

    cs = cs_ref[...]
    sn = sn_ref[...]
    rot = lambda x: x * cs + pltpu.roll(x, HEAD_DIM // 2, 1) * sn
    r2 = lax.broadcasted_iota(jnp.int32, (c, c), 0)
    c2 = lax.broadcasted_iota(jnp.int32, (c, c), 1)
    rel = (r2 - c2).astype(f32)
    idx = lax.broadcasted_iota(jnp.int32, (c, 1), 0)
    idx_f = idx.astype(f32)
    valid = idx < n_valid

    for h in range(H_RET):
        lg = math.log(1.0 - 2.0 ** (-5.0 - h))
        sl = slice(h * HEAD_DIM, (h + 1) * HEAD_DIM)
        dmat = jnp.where(rel >= 0, jnp.exp(jnp.maximum(rel, 0.0) * lg), 0.0)
        xi = jnp.exp((idx_f + 1.0) * lg)
        zeta = jnp.where(valid, jnp.exp((c_len - 1.0 - idx_f) * lg), 0.0)
        for s in range(n_seq):
            q = rot(q_ref[s, :, sl])
            k = rot(k_ref[s, :, sl]) * HEAD_DIM ** -0.5
            v = v_ref[s, :, sl]
            st = s_scr[s, h]
            o = _dot(_dot_nt(q, k) * dmat, v) + _dot(q * xi, st)
            s_scr[s, h] = math.exp(c_len * lg) * st + _dot_tn(k * zeta, v)
            o = o * lax.rsqrt(jnp.mean(o * o, axis=-1, keepdims=True) + RMS_EPS)
            o_ref[s, :, sl] = o * _silu(g_ref[s, :, sl])

    @pl.when(c_idx == pl.num_programs(1) - 1)
    def _():
        sfin_ref[...] = s_scr[...]


def retention_mixer(qkvg, cs, sn, s0, chunk, n_valid, n_seq):
    proj, base = qkvg
    s_n, t, _ = proj.shape
    assert s_n % n_seq == 0 and t % chunk == 0
    col_spec = lambda part: pl.BlockSpec((n_seq, chunk, W_RET), lambda s, c: (s, c, base + part))
    return pl.pallas_call(
        functools.partial(_ret_kernel, n_valid=n_valid),
        grid=(s_n // n_seq, t // chunk),
        in_specs=[col_spec(0), col_spec(1), col_spec(2), col_spec(3),
                  pl.BlockSpec((chunk, HEAD_DIM), lambda s, c: (c, 0)),
                  pl.BlockSpec((chunk, HEAD_DIM), lambda s, c: (c, 0)),
                  pl.BlockSpec((None, n_seq, H_RET, HEAD_DIM, HEAD_DIM), lambda s, c: (s0[1], s, 0, 0, 0))],
        out_specs=[pl.BlockSpec((n_seq, chunk, W_RET), lambda s, c: (s, c, 0)),
                   pl.BlockSpec((n_seq, H_RET, HEAD_DIM, HEAD_DIM), lambda s, c: (s, 0, 0, 0))],
        out_shape=[jax.ShapeDtypeStruct((s_n, t, W_RET), f32),
                   jax.ShapeDtypeStruct((s_n, H_RET, HEAD_DIM, HEAD_DIM), f32)],
        scratch_shapes=[pltpu.VMEM((n_seq, H_RET, HEAD_DIM, HEAD_DIM), f32)],
        compiler_params=_cparams("arbitrary", "arbitrary"),
        name="retention_mixer",
    )(proj, proj, proj, proj, cs, sn, s0[0])


def _rotary_tables(pos):
    half = HEAD_DIM // 2
    inv = 1.0 / (10000.0 ** jnp.linspace(0.0, 1.0, half, dtype=f32))
    ang = pos.astype(f32)[:, None] * inv[None, :]
    cos, sin = jnp.cos(ang), jnp.sin(ang)
    return jnp.concatenate([cos, cos], axis=1), jnp.concatenate([-sin, sin], axis=1)


def _t5_bias_shifted(dist, tab):
    val = tab(0)
    for b in range(1, NUM_BUCKETS):
        val = jnp.where(dist >= _BUCKET_START[b], tab(b), val)
    return val - tab(NUM_BUCKETS - 1)


def _top_blocks(gate, n_sel):
    lane = lax.broadcasted_iota(jnp.int32, gate.shape, 1).astype(f32)
    work = gate
    for _ in range(n_sel):
        m = jnp.max(work, axis=1, keepdims=True)
        first = jnp.min(jnp.where(work == m, lane, float(gate.shape[1])), axis=1, keepdims=True)
        work = jnp.where((lane == first) & (m > 0.5 * NEG_INF), NEG_INF, work)
    return work < gate


def _scores(q_aug, k_aug):
    return lax.dot_general(q_aug, k_aug, (((1,), (1,)), ((), ())), preferred_element_type=f32)


def _moba_prompt_kernel(tab_ref, q_ref, k_ref, v_ref, o_ref,
                        kaug_scr, v_scr, kmean_scr, bias_scr, s_scr, l_scr, acc_scr):
    h = pl.program_id(0)
    i = pl.program_id(1)
    blk = MOBA_BLOCK
    t = k_ref.shape[0]
    n_blocks = t // blk
    ck = FAR_CHUNK * blk
    masked_lane = LANES - 1

    @pl.when(i == 0)
    def _():
        k = k_ref[...]
        kmean_scr[...] = jnp.zeros(kmean_scr.shape, f32)
        kmean_scr[0:n_blocks, :] = jnp.mean(k.reshape(n_blocks, blk, HEAD_DIM), axis=1)
        row = lax.broadcasted_iota(jnp.int32, (t, LANES), 0)
        lane_t = lax.broadcasted_iota(jnp.int32, (t, LANES), 1)
        kaug_scr[blk:blk + t, 0:HEAD_DIM] = k.astype(bf16)
        kaug_scr[blk:blk + t, HEAD_DIM:] = jnp.where(lane_t == row // blk, 1.0, 0.0).astype(bf16)
        v_scr[blk:blk + t, :] = v_ref[...].astype(bf16)
        for start, size in ((0, blk), (blk + t, kaug_scr.shape[0] - blk - t)):
            if not size:
                continue
            lane_p = lax.broadcasted_iota(jnp.int32, (size, LANES), 1)
            kaug_scr[start:start + size, 0:HEAD_DIM] = jnp.zeros((size, HEAD_DIM), bf16)
            kaug_scr[start:start + size, HEAD_DIM:] = jnp.where(lane_p == masked_lane, 1.0, 0.0).astype(bf16)
            v_scr[start:start + size, :] = jnp.zeros((size, HEAD_DIM), bf16)
        r2 = lax.broadcasted_iota(jnp.int32, (blk, blk), 0)
        c2 = lax.broadcasted_iota(jnp.int32, (blk, blk), 1)
        tab = lambda b: tab_ref[b, h]
        bias_scr[0] = _t5_bias_shifted(r2 - c2 + blk, tab)
        bias_scr[1] = jnp.where(r2 >= c2, _t5_bias_shifted(r2 - c2, tab), NEG_INF)

    q = q_ref[...]
    lane = lax.broadcasted_iota(jnp.int32, (blk, LANES), 1)
    gate = jnp.where(lane < i, _dot3_nt(q, kmean_scr[...]), NEG_INF)
    sel = _top_blocks(gate, MOBA_TOPK)
    qs = q * HEAD_DIM ** -0.5
    neg_far = jnp.where(sel & (lane < i - 1), 0.0, NEG_INF)
    neg_near = jnp.where((sel & (lane == i - 1)) | (lane == i), 0.0, NEG_INF)
    q_far = jnp.concatenate([qs, neg_far], axis=1).astype(bf16)
    q_near = jnp.concatenate([qs, neg_near], axis=1).astype(bf16)
    n_chunks = (jnp.maximum(i - 1, 0) + FAR_CHUNK - 1) // FAR_CHUNK

    far_rows = lambda c: pl.ds(pl.multiple_of(blk + c * ck, blk), ck)

    def scores_pass(c, mx):
        s = _scores(q_far, kaug_scr[far_rows(c), :])
        s_scr[c] = s
        return jnp.maximum(mx, _fold(s, jnp.maximum))

    mx = lax.fori_loop(0, n_chunks, scores_pass, jnp.full((blk, LANES), NEG_INF, f32))
    near_rows = pl.ds(pl.multiple_of(i * blk, blk), 2 * blk)
    s_near = _scores(q_near, kaug_scr[near_rows, :]) + jnp.concatenate([bias_scr[0], bias_scr[1]], axis=1)
    m = jnp.max(jnp.maximum(mx, _fold(s_near, jnp.maximum)), axis=1, keepdims=True)
    p_near = jnp.exp(s_near - m)
    l_scr[...] = _fold(p_near, jnp.add)
    acc_scr[...] = jnp.dot(p_near.astype(bf16), v_scr[near_rows, :], preferred_element_type=f32)

    def values_pass(c, carry):
        p = jnp.exp(s_scr[c] - m)
        l_scr[...] += _fold(p, jnp.add)
        acc_scr[...] += jnp.dot(p.astype(bf16), v_scr[far_rows(c), :], preferred_element_type=f32)
        return carry

    lax.fori_loop(0, n_chunks, values_pass, 0)
    o_ref[...] = acc_scr[...] / jnp.sum(l_scr[...], axis=1, keepdims=True)


def moba_prompt(q, k, v, rel_bias):
    t = q[0].shape[0]
    blk = MOBA_BLOCK
    n_blocks = t // blk
    assert t % blk == 0 and n_blocks < LANES - 1 and blk >= _BUCKET_START[-1]
    qb, kb, vb = q[1], k[1], v[1]
    far_max = -(-max(n_blocks - 2, 1) // FAR_CHUNK) * FAR_CHUNK
    kv_rows = (1 + max(n_blocks, far_max + 1)) * blk
    return pl.pallas_call(
        _moba_prompt_kernel,
        grid=(H_MOBA, n_blocks),
        in_specs=[pl.BlockSpec(memory_space=pltpu.SMEM),
                  pl.BlockSpec((blk, HEAD_DIM), lambda h, i: (i, qb + h)),
                  pl.BlockSpec((t, HEAD_DIM), lambda h, i: (0, kb + h)),
                  pl.BlockSpec((t, HEAD_DIM), lambda h, i: (0, vb + h))],
        out_specs=pl.BlockSpec((blk, HEAD_DIM), lambda h, i: (i, h)),
        out_shape=jax.ShapeDtypeStruct((t, W_MOBA), f32),
        scratch_shapes=[pltpu.VMEM((kv_rows, 2 * HEAD_DIM), bf16),
                        pltpu.VMEM((kv_rows, HEAD_DIM), bf16),
                        pltpu.VMEM((LANES, HEAD_DIM), f32),
                        pltpu.VMEM((2, blk, blk), f32),
                        pltpu.VMEM((far_max // FAR_CHUNK, blk, FAR_CHUNK * blk), f32),
                        pltpu.VMEM((blk, LANES), f32),
                        pltpu.VMEM((blk, HEAD_DIM), f32)],
        compiler_params=_cparams("arbitrary", "arbitrary"),
        name="moba_prompt",
    )(rel_bias, q[0], k[0], v[0])


def _moba_sample_kernel(pt_ref, q_ref, kn_ref, vn_ref, tabr_ref, *rest, n_pages, n_new):
    kp = rest[:n_pages]
    vp = rest[n_pages:2 * n_pages]
    o_ref, onehot_scr, near_scr, own_scr, s_scr = rest[2 * n_pages:]
    page = kp[0].shape[0]
    pr = page * H_MOBA
    past = n_pages * page
    ppb = MOBA_BLOCK // page
    n_blocks = n_pages // ppb
    rows = H_MOBA * n_new
    n_own = kn_ref.shape[0]
    near = [p for p in range(n_pages) if past - (p * page + page - 1) < _BUCKET_START[-1]]

    r_l = lax.broadcasted_iota(jnp.int32, (rows, LANES), 0)
    l_l = lax.broadcasted_iota(jnp.int32, (rows, LANES), 1)
    head_r = r_l // n_new
    t_r = r_l % n_new

    @pl.when(pl.program_id(0) == 0)
    def _():
        kr = lax.broadcasted_iota(jnp.int32, (pr, LANES), 0)
        kl = lax.broadcasted_iota(jnp.int32, (pr, LANES), 1)
        for n in range(n_blocks):
            onehot_scr[n] = jnp.where(kl == n * H_MOBA + kr % H_MOBA, 1.0, 0.0).astype(bf16)
        tabr = tabr_ref[...]
        tab = lambda b: tabr[:, b:b + 1]
        rr = lax.broadcasted_iota(jnp.int32, (rows, pr), 0)
        cc = lax.broadcasted_iota(jnp.int32, (rows, pr), 1)
        for idx, p in enumerate(near):
            near_scr[idx] = _t5_bias_shifted(past + rr % n_new - (p * page + cc // H_MOBA), tab)
        t_k = l_l // H_MOBA
        visible = (l_l % H_MOBA == head_r) & (t_k <= t_r) & (t_k < n_new)
        own_scr[...] = jnp.where(visible, _t5_bias_shifted(t_r - t_k, tab), NEG_INF)

    q = q_ref[...]
    kmean = []
    for n in range(n_blocks):
        tot = functools.reduce(jnp.add, [jnp.sum(kp[p][...], axis=0) for p in range(n * ppb, (n + 1) * ppb)])
        kmean.append(tot * (1.0 / MOBA_BLOCK))
    kmean.append(jnp.zeros((LANES - n_blocks * H_MOBA, HEAD_DIM), f32))
    gate = _dot3_nt(q, jnp.concatenate(kmean, axis=0))
    own_head = (l_l < n_blocks * H_MOBA) & (l_l % H_MOBA == head_r)
    sel = _top_blocks(jnp.where(own_head, gate, NEG_INF), min(MOBA_TOPK, n_blocks))
    qs = q * HEAD_DIM ** -0.5
    q_aug = jnp.concatenate([qs, jnp.where(sel, 0.0, NEG_INF)], axis=1).astype(bf16)

    mx = jnp.full((rows, LANES), NEG_INF, f32)
    for p in range(n_pages):
        k_aug = jnp.concatenate([kp[p][...].reshape(pr, HEAD_DIM).astype(bf16), onehot_scr[p // ppb]], axis=1)
        s = _scores(q_aug, k_aug)
        if p in near:
            s = s + near_scr[near.index(p)]
        s_scr[:, p * pr:(p + 1) * pr] = s
        mx = jnp.maximum(mx, _fold(s, jnp.maximum))
    pad_own = lambda r: jnp.concatenate([r[...], jnp.zeros((LANES - n_own, HEAD_DIM), f32)], axis=0)
    s_own = _dot_nt(qs, pad_own(kn_ref)) + own_scr[...]
    m = jnp.max(jnp.maximum(mx, s_own), axis=1, keepdims=True)
    p_own = jnp.exp(s_own - m)
    lsum = p_own
    acc = _dot(p_own, pad_own(vn_ref))
    for p in range(n_pages):
        pe = jnp.exp(s_scr[:, p * pr:(p + 1) * pr] - m)
        lsum = lsum + _fold(pe, jnp.add)
        acc = acc + _dot(pe, vp[p][...].reshape(pr, HEAD_DIM))
    o_ref[...] = acc / jnp.sum(lsum, axis=1, keepdims=True)


def moba_sample(q, k_new, v_new, cache_k, cache_v, page_table, rel_bias, layer, n_new):
    bsz, rows, _ = q.shape
    n_pages = page_table.shape[1]
    page = cache_k.shape[2]
    past = n_pages * page
    n_blocks = past // MOBA_BLOCK
    assert past % MOBA_BLOCK == 0 and MOBA_BLOCK % page == 0 and n_blocks * H_MOBA <= LANES
    assert rows == H_MOBA * n_new and k_new.shape[1] <= LANES
    n_near = len([p for p in range(n_pages) if past - (p * page + page - 1) < _BUCKET_START[-1]])
    tab_rows = jnp.pad(jnp.repeat(rel_bias.T, n_new, axis=0), ((0, 0), (0, LANES - NUM_BUCKETS)))
    page_spec = lambda p: pl.BlockSpec((None, None, page, H_MOBA, HEAD_DIM),
                                       lambda b, pt: (layer, pt[b, p], 0, 0, 0))
    row_spec = lambda a: pl.BlockSpec((None,) + a.shape[1:], lambda b, pt: (b, 0, 0))
    grid_spec = pltpu.PrefetchScalarGridSpec(
        num_scalar_prefetch=1,
        grid=(bsz,),
        in_specs=([row_spec(q), row_spec(k_new), row_spec(v_new),
                   pl.BlockSpec(tab_rows.shape, lambda b, pt: (0, 0))]
                  + [page_spec(p) for p in range(n_pages)] * 2),
        out_specs=row_spec(q),
        scratch_shapes=[pltpu.VMEM((n_blocks, page * H_MOBA, LANES), bf16),
                        pltpu.VMEM((max(n_near, 1), rows, page * H_MOBA), f32),
                        pltpu.VMEM((rows, LANES), f32),
                        pltpu.VMEM((rows, past * H_MOBA), f32)],
    )
    return pl.pallas_call(
        functools.partial(_moba_sample_kernel, n_pages=n_pages, n_new=n_new),
        grid_spec=grid_spec,
        out_shape=jax.ShapeDtypeStruct(q.shape, f32),
        compiler_params=_cparams("arbitrary"),
        name="moba_sample",
    )(page_table, q, k_new, v_new, tab_rows, *([cache_k] * n_pages), *([cache_v] * n_pages))


def _post(x, y, gain, gate):
    return x + gate * (y * lax.rsqrt(jnp.mean(y * y, axis=-1, keepdims=True) + RMS_EPS) * gain)


def _out_proj_kernel(oa_ref, ob_ref, oc_ref, w_ref, x_ref, gt_ref, g_ref, o_ref):
    y = (jnp.dot(oa_ref[...].astype(bf16), w_ref[0:W_GDN, :], preferred_element_type=f32)
         + jnp.dot(ob_ref[...].astype(bf16), w_ref[W_GDN:W_GDN + W_MOBA, :], preferred_element_type=f32)
         + jnp.dot(oc_ref[...].astype(bf16), w_ref[W_GDN + W_MOBA:, :], preferred_element_type=f32))
    o_ref[...] = _post(x_ref[...], y, g_ref[...], gt_ref[...])


def out_proj_residual(o_a, o_b, o_c, w_out, layer, x, mod, gt_blk, gain, tm):
    m, d = x.shape
    per_row = mod.shape[0] != 1
    mod_spec = pl.BlockSpec((tm if per_row else 1, d), (lambda i: (i, gt_blk)) if per_row else (lambda i: (0, gt_blk)))
    return pl.pallas_call(
        _out_proj_kernel,
        grid=(m // tm,),
        in_specs=[pl.BlockSpec((tm, W_GDN), lambda i: (i, 0)),
                  pl.BlockSpec((tm, W_MOBA), lambda i: (i, 0)),
                  pl.BlockSpec((tm, W_RET), lambda i: (i, 0)),
                  pl.BlockSpec((None,) + w_out.shape[1:], lambda i: (layer, 0, 0)),
                  pl.BlockSpec((tm, d), lambda i: (i, 0)),
                  mod_spec,
                  pl.BlockSpec((1, d), lambda i: (0, 0))],
        out_specs=pl.BlockSpec((tm, d), lambda i: (i, 0)),
        out_shape=jax.ShapeDtypeStruct((m, d), f32),
        compiler_params=_cparams("arbitrary"),
        name="out_proj_residual",
    )(o_a, o_b, o_c, w_out, x, mod, gain.reshape(1, d))


def _ffn_down_kernel(ug_ref, uu_ref, pg_ref, pu_ref, hg_ref, hu_ref, cwg_ref, cwu_ref, wd_ref,
                     x_ref, gt_ref, g_ref, o_ref, acc_ref, *, row_shift):
    i = pl.program_id(0)
    f = pl.program_id(1)
    hdr = hg_ref.shape[0]
    first = i == 0

    def conv(u_ref, p_ref, h_ref, cw_ref):
        head = jnp.where(first, h_ref[...], p_ref[...])
        ext = jnp.concatenate([head, u_ref[...]], axis=0)
        out = ext[hdr:] * cw_ref[FFN_CONV - 1:FFN_CONV, :]
        for j in range(1, FFN_CONV):
            out = out + pltpu.roll(ext, j * row_shift, 0)[hdr:] * cw_ref[FFN_CONV - 1 - j:FFN_CONV - j, :]
        return out

    act = _silu(conv(ug_ref, pg_ref, hg_ref, cwg_ref)) * conv(uu_ref, pu_ref, hu_ref, cwu_ref)
    part = jnp.dot(act.astype(bf16), wd_ref[...], preferred_element_type=f32)

    @pl.when(f == 0)
    def _():
        acc_ref[...] = part

    @pl.when(f > 0)
    def _():
        acc_ref[...] += part

    @pl.when(f == pl.num_programs(1) - 1)
    def _():
        o_ref[...] = _post(x_ref[...], acc_ref[...], g_ref[...], gt_ref[...])


def ffn_down_residual(u, hist, conv_w, w_down, layer, x, mod, gt_blk, gain, tm, tf, row_shift):
    m, d = x.shape
    ff = w_down.shape[1]
    nf = ff // tf
    hdr = hist.shape[0]
    assert hdr >= (FFN_CONV - 1) * row_shift and tm % hdr == 0
    per_row = mod.shape[0] != 1
    mod_spec = pl.BlockSpec((tm if per_row else 1, d),
                            (lambda i, f: (i, gt_blk)) if per_row else (lambda i, f: (0, gt_blk)))
    prev = lambda off: (lambda i, f: (jnp.maximum(i * (tm // hdr) - 1, 0), f + off))
    return pl.pallas_call(
        functools.partial(_ffn_down_kernel, row_shift=row_shift),
        grid=(m // tm, nf),
        in_specs=[pl.BlockSpec((tm, tf), lambda i, f: (i, f)),
                  pl.BlockSpec((tm, tf), lambda i, f: (i, f + nf)),
                  pl.BlockSpec((hdr, tf), prev(0)),
                  pl.BlockSpec((hdr, tf), prev(nf)),
                  pl.BlockSpec((hdr, tf), lambda i, f: (0, f)),
                  pl.BlockSpec((hdr, tf), lambda i, f: (0, f + nf)),
                  pl.BlockSpec((FFN_CONV, tf), lambda i, f: (0, f)),
                  pl.BlockSpec((FFN_CONV, tf), lambda i, f: (0, f + nf)),
                  pl.BlockSpec((None, tf, d), lambda i, f: (layer, f, 0)),
                  pl.BlockSpec((tm, d), lambda i, f: (i, 0)),
                  mod_spec,
                  pl.BlockSpec((1, d), lambda i, f: (0, 0))],
        out_specs=pl.BlockSpec((tm, d), lambda i, f: (i, 0)),
        out_shape=jax.ShapeDtypeStruct((m, d), f32),
        scratch_shapes=[pltpu.VMEM((tm, d), f32)],
        compiler_params=_cparams("arbitrary", "arbitrary"),
        name="ffn_down_residual",
    )(u, u, u, u, hist, hist, conv_w, conv_w, w_down, x, mod, gain.reshape(1, d))


def _permute_w_in(w_in):
    s1 = 3 * W_GDN
    s2 = s1 + W_GDN
    s4 = s2 + 2 * H_GDN
    ab = w_in[..., s2:s4]
    pad = jnp.zeros(w_in.shape[:-1] + (PROJ_W - COL_AB - 2 * H_GDN,), w_in.dtype)
    return jnp.concatenate([w_in[..., :s2], w_in[..., s4:], ab, pad], axis=-1).astype(bf16)


def kernel(x_prompt, x_sample, c_prompt, c_sample, cache_k, cache_v, page_table, state_gdn_conv, state_gdn, state_ret, state_ffn_conv, w_ada, b_ada, g_pre_mix, g_post_mix, g_pre_ffn, g_post_ffn, w_in, gdn_conv_w, gdn_a_log, gdn_dt_bias, gdn_norm_w, rel_bias, w_out, w_up, ffn_conv_w, w_down):
    bp, seq, d = x_prompt.shape
    bs, dec = x_sample.shape[:2]
    depth = w_in.shape[0]
    assert bp == 1 and dec <= SUBLANES
    past_len = page_table.shape[1] * cache_k.shape[2]
    d_ff = w_down.shape[1]
    pad_t = SUBLANES - dec
    seq_per_step = MXU_DIM // (H_GDN * SUBLANES)

    w_in_b = _permute_w_in(w_in)
    w_out_b = w_out.astype(bf16)
    w_up_b = w_up.astype(bf16)
    w_down_b = w_down.astype(bf16)

    n_c = bp + bs
    c_all = jnp.concatenate([c_prompt, c_sample, jnp.zeros((-n_c % SUBLANES, d), f32)], axis=0)
    mod = ada_modulation(c_all, w_ada, b_ada)

    cs_p, sn_p = _rotary_tables(jnp.arange(seq, dtype=jnp.int32))
    cs_s, sn_s = _rotary_tables(past_len + jnp.arange(SUBLANES, dtype=jnp.int32))

    yp = x_prompt.reshape(seq, d)
    ys = jnp.swapaxes(x_sample, 0, 1).reshape(dec * bs, d)
    to_bm = lambda a: jnp.pad(jnp.swapaxes(a.reshape(dec, bs, -1), 0, 1), ((0, 0), (0, pad_t), (0, 0)))
    to_tm = lambda a: jnp.swapaxes(a[:, :dec], 0, 1).reshape(dec * bs, -1)
    heads_tm = lambda a: a.reshape(dec, bs, H_MOBA, HEAD_DIM)

    assert H_GDN == H_RET
    zero_state = jnp.zeros((1, bp, H_GDN, HEAD_DIM, HEAD_DIM), f32)

    outs = [[] for _ in range(12)]
    for l in range(depth):
        mod_p = mod[l, 0:bp]
        mod_s = jnp.tile(mod[l, bp:bp + bs], (dec, 1))

        h_p = norm_modulate(yp, g_pre_mix[l], mod_p, 1, 0, tm=1024)
        qkva_p, z_p, qb_p, kb_p, vb_p, c_p, ab_p = matmul_split(
            h_p, w_in_b, l, tm=1024, tn=SPLIT_TILE,
            widths=(3 * W_GDN, W_GDN, W_MOBA, W_MOBA, W_MOBA, 4 * W_RET, PROJ_W - COL_AB))
        oa_p, gs_p = gdn_mixer((qkva_p[None], 0), (z_p[None], 0), (ab_p[None], 0),
                               jnp.zeros((1, SUBLANES, 3 * W_GDN), f32), (zero_state, 0),
                               gdn_conv_w[l], gdn_a_log[l], gdn_dt_bias[l], gdn_norm_w[l],
                               chunk=GDN_CHUNK, n_valid=GDN_CHUNK, n_seq=1, n_sub=2)
        oc_p, rs_p = retention_mixer((c_p[None], 0), cs_p, sn_p, (zero_state, 0),
                                     chunk=RET_CHUNK, n_valid=RET_CHUNK, n_seq=1)
        ob_p = moba_prompt((qb_p, 0), (kb_p, 0), (vb_p, 0), rel_bias)
        yp = out_proj_residual(oa_p[0], ob_p, oc_p[0], w_out_b, l, yp, mod_p, 2, g_post_mix[l], tm=512)

        proj_s = norm_mod_matmul(ys, g_pre_mix[l], mod_s, 1, 0, w_in_b, l, tm=dec * bs, tn=PROJ_TILE)
        proj_sb = to_bm(proj_s)
        hist_s = jnp.pad(state_gdn_conv[l], ((0, 0), (SUBLANES - (GDN_CONV - 1), 0), (0, 0)))
        oa_s, gs_s = gdn_mixer((proj_sb, COL_QKV_A // (3 * W_GDN)), (proj_sb, COL_Z_A // W_GDN),
                               (proj_sb, COL_AB // LANES), hist_s, (state_gdn, l), gdn_conv_w[l], gdn_a_log[l],
                               gdn_dt_bias[l], gdn_norm_w[l], chunk=SUBLANES, n_valid=dec, n_seq=seq_per_step,
                               n_sub=1)
        oc_s, rs_s = retention_mixer((proj_sb, COL_QKVG_C // W_RET), cs_s, sn_s, (state_ret, l), chunk=SUBLANES,
                                     n_valid=dec, n_seq=seq_per_step)
        q_s, k_s, v_s = [heads_tm(proj_s[:, COL_QKV_B + j * W_MOBA:COL_QKV_B + (j + 1) * W_MOBA]) for j in range(3)]
        k_s, v_s = jnp.swapaxes(k_s, 0, 1), jnp.swapaxes(v_s, 0, 1)
        ob_s = moba_sample(jnp.transpose(q_s, (1, 2, 0, 3)).reshape(bs, H_MOBA * dec, HEAD_DIM),
                           k_s.reshape(bs, dec * H_MOBA, HEAD_DIM), v_s.reshape(bs, dec * H_MOBA, HEAD_DIM),
                           cache_k, cache_v, page_table, rel_bias, l, dec)
        ob_s = jnp.transpose(ob_s.reshape(bs, H_MOBA, dec, HEAD_DIM), (2, 0, 1, 3)).reshape(dec * bs, W_MOBA)
        ys = out_proj_residual(to_tm(oa_s), ob_s, to_tm(oc_s), w_out_b, l, ys, mod_s, 2,
                               g_post_mix[l], tm=dec * bs)

        u_p = norm_mod_matmul(yp, g_pre_ffn[l], mod_p, 4, 3, w_up_b, l, tm=1024, tn=1024)
        yp = ffn_down_residual(u_p, jnp.zeros((SUBLANES, 2 * d_ff), f32), ffn_conv_w[l], w_down_b, l, yp,
                               mod_p, 5, g_post_ffn[l], tm=512, tf=d_ff // 4, row_shift=1)
        u_s = norm_mod_matmul(ys, g_pre_ffn[l], mod_s, 4, 3, w_up_b, l, tm=dec * bs, tn=1024)
        hist_f = jnp.swapaxes(state_ffn_conv[l], 0, 1).reshape((FFN_CONV - 1) * bs, 2 * d_ff)
        ys = ffn_down_residual(u_s, hist_f, ffn_conv_w[l], w_down_b, l, ys, mod_s, 5, g_post_ffn[l],
                               tm=dec * bs, tf=512, row_shift=bs)

        outs[0].append(kb_p.reshape(bp, seq, H_MOBA, HEAD_DIM))
        outs[1].append(vb_p.reshape(bp, seq, H_MOBA, HEAD_DIM))
        outs[2].append(k_s)
        outs[3].append(v_s)
        outs[4].append(qkva_p[seq - (GDN_CONV - 1):].reshape(bp, GDN_CONV - 1, 3 * W_GDN))
        outs[5].append(proj_sb[:, dec - (GDN_CONV - 1):dec, :3 * W_GDN])
        outs[6].append(gs_p)
        outs[7].append(gs_s)
        outs[8].append(rs_p)
        outs[9].append(rs_s)
        outs[10].append(u_p[seq - (FFN_CONV - 1):].reshape(bp, FFN_CONV - 1, 2 * d_ff))
        outs[11].append(jnp.swapaxes(u_s.reshape(dec, bs, 2 * d_ff)[dec - (FFN_CONV - 1):], 0, 1))

    y_prompt = yp.reshape(bp, seq, d)
    y_sample = jnp.swapaxes(ys.reshape(dec, bs, d), 0, 1)
    return (y_prompt, y_sample) + tuple(jnp.stack(o, axis=0) for o in outs)
```
